```python
import jax, jax.numpy as jnp
from jax import lax
import numpy as np

D_MODEL = 1024
BATCH = 8
SEQ = 4096
DEPTH = 2
DEC_BATCH = 32
DEC_SEQ = 1
PAST_LEN = 16384
PAGE_SIZE = 128

HEAD_DIM = 64
NSA_HEADS = 8
NSA_KV_HEADS = 2
NSA_GROUP = NSA_HEADS // NSA_KV_HEADS
CMP_STRIDE = 16
CMP_LEN = 32
CMP_HIDDEN = 128
SLC_BLOCK = 64
SLC_TOP_N = 16
WINDOW = 512
SB_HEADS = 8
D_FF = 4 * D_MODEL
ROPE_THETA = 10000.0
RMS_EPS = 1e-6
FORCE_BONUS = 1e4
NSA_Q_BLOCK = 64
SB_Q_BLOCK = 128
NSA_Q_W = NSA_HEADS * HEAD_DIM
NSA_KV_W = NSA_KV_HEADS * HEAD_DIM
SB_W = SB_HEADS * HEAD_DIM
IN_SIZES = (NSA_Q_W, NSA_KV_W, NSA_KV_W, NSA_KV_W, NSA_KV_W, NSA_KV_W, NSA_KV_W, 3 * NSA_HEADS, SB_W, SB_W, SB_W)
IN_COLS = NSA_Q_W + 6 * NSA_KV_W + 3 * NSA_HEADS + 3 * SB_W

kernel_name = "nsa_stickbreaking_gated_hybrid_step"


def rmsnorm(x, g):
    xf = x.astype(jnp.float32)
    y = xf * lax.rsqrt(jnp.mean(xf * xf, axis=-1, keepdims=True) + RMS_EPS)
    return (y * g.astype(jnp.float32)).astype(x.dtype)


def rope(x, pos):
    half = HEAD_DIM // 2
    inv = ROPE_THETA ** (-jnp.arange(half, dtype=jnp.float32) / half)
    ang = pos.astype(jnp.float32)[:, None] * inv[None, :]
    cos = jnp.cos(ang)[:, None, :]
    sin = jnp.sin(ang)[:, None, :]
    xf = x.astype(jnp.float32)
    x1, x2 = xf[..., :half], xf[..., half:]
    return jnp.concatenate([x1 * cos - x2 * sin, x2 * cos + x1 * sin], axis=-1).astype(x.dtype)


def masked_softmax(s, mask):
    s = jnp.where(mask, s, -jnp.inf)
    m = jnp.max(s, axis=-1, keepdims=True)
    m = jnp.where(jnp.isfinite(m), m, 0.0)
    p = jnp.where(mask, jnp.exp(s - m), 0.0)
    return p / jnp.maximum(jnp.sum(p, axis=-1, keepdims=True), 1e-30)


def project(xn, pos, w_in, q_norm, k_norm):
    b, L, _ = xn.shape
    h = xn @ w_in
    cuts = np.cumsum(IN_SIZES)[:-1].tolist()
    q, kc, vc, ks, vs, kw, vw, g, sq, sk, sv = jnp.split(h, cuts, axis=-1)
    kvh = lambda t: t.reshape(b, L, NSA_KV_HEADS, HEAD_DIM)
    sbh = lambda t: t.reshape(b, L, SB_HEADS, HEAD_DIM)
    q = rope(rmsnorm(q.reshape(b, L, NSA_HEADS, HEAD_DIM), q_norm), pos)
    kc = rope(rmsnorm(kvh(kc), k_norm[0]), pos)
    ks = rope(rmsnorm(kvh(ks), k_norm[1]), pos)
    kw = rope(rmsnorm(kvh(kw), k_norm[2]), pos)
    gates = jax.nn.sigmoid(g).reshape(b, L, NSA_HEADS, 3)
    return q, kc, kvh(vc), ks, kvh(vs), kw, kvh(vw), gates, sbh(sq), sbh(sk), sbh(sv)


def compress(k, pos_emb, w1, b1, w2):
    b, L, g, d = k.shape
    nc = L // CMP_STRIDE
    r = CMP_LEN // CMP_STRIDE
    ch = k[:, :nc * CMP_STRIDE].reshape(b, nc, CMP_STRIDE, g, d)
    nblk = nc - r + 1
    blk = jnp.concatenate([ch[:, i:i + nblk] for i in range(r)], axis=2)
    blk = blk + pos_emb[None, None, :, None, :]
    flat = blk.transpose(0, 1, 3, 2, 4).reshape(b, nblk, g, CMP_LEN * d)
    return jax.nn.silu(flat @ w1 + b1) @ w2


def nsa_keys(kc, vc, ks, vs, cmp_pos, cmp_w1, cmp_b1, cmp_w2):
    k_cb = compress(kc, cmp_pos[0], cmp_w1[0], cmp_b1[0], cmp_w2[0])
    v_cb = compress(vc, cmp_pos[1], cmp_w1[1], cmp_b1[1], cmp_w2[1])
    cb_end = jnp.arange(k_cb.shape[1], dtype=jnp.int32) * CMP_STRIDE + (CMP_LEN - 1)
    b, L, g, d = ks.shape
    n_sel = -(-L // SLC_BLOCK)

    def blocks(t):
        t = jnp.pad(t, ((0, 0), (0, n_sel * SLC_BLOCK - L), (0, 0), (0, 0)))
        return t.reshape(b, n_sel, SLC_BLOCK, g, d).transpose(0, 3, 1, 2, 4)

    return k_cb, v_cb, cb_end, blocks(ks), blocks(vs)


def nsa_block(q, q_pos, gates, k_cb, v_cb, cb_end, k_sb, v_sb, k_w, v_w, w_pos):
    b, qb = q.shape[0], q.shape[1]
    scale = HEAD_DIM ** -0.5
    qg = q.reshape(b, qb, NSA_KV_HEADS, NSA_GROUP, HEAD_DIM)
    s_c = jnp.einsum('bqgrd,bjgd->bgrqj', qg, k_cb).astype(jnp.float32) * scale
    p_c = masked_softmax(s_c, cb_end[None, :] <= q_pos[:, None])
    o_c = jnp.einsum('bgrqj,bjgd->bqgrd', p_c.astype(v_cb.dtype), v_cb)
    r = CMP_LEN // CMP_STRIDE
    ratio = SLC_BLOCK // CMP_STRIDE
    imp = jnp.sum(p_c, axis=2)
    pad_last = lambda t, lo, hi: jnp.pad(t, ((0, 0), (0, 0), (0, 0), (lo, hi)))
    chunk = sum(pad_last(imp, i, r - 1 - i) for i in range(r))
    n_sel = k_sb.shape[2]
    chunk = pad_last(chunk, 0, n_sel * ratio - chunk.shape[-1])
    imp_s = chunk.reshape(b, NSA_KV_HEADS, qb, n_sel, ratio).sum(-1)
    blk = jnp.arange(n_sel, dtype=jnp.int32)
    cur = (q_pos // SLC_BLOCK)[:, None]
    forced = (blk == 0) | (blk == cur) | (blk == cur - 1)
    valid = blk * SLC_BLOCK <= q_pos[:, None]
    score = jnp.where(valid, imp_s + FORCE_BONUS * forced.astype(jnp.float32), -jnp.inf)
    n_top = min(SLC_TOP_N, n_sel)
    _, idx = lax.top_k(score, n_top)
    bi = jnp.arange(b)[:, None, None, None]
    gi = jnp.arange(NSA_KV_HEADS)[None, :, None, None]
    n_keys = n_top * SLC_BLOCK
    ks = k_sb[bi, gi, idx].reshape(b, NSA_KV_HEADS, qb, n_keys, HEAD_DIM)
    vs = v_sb[bi, gi, idx].reshape(b, NSA_KV_HEADS, qb, n_keys, HEAD_DIM)
    pos_s = (idx[..., None] * SLC_BLOCK + jnp.arange(SLC_BLOCK, dtype=jnp.int32)).reshape(b, NSA_KV_HEADS, qb, n_keys)
    s_s = jnp.einsum('bqgrd,bgqsd->bgrqs', qg, ks).astype(jnp.float32) * scale
    p_s = masked_softmax(s_s, (pos_s <= q_pos[:, None])[:, :, None])
    o_s = jnp.einsum('bgrqs,bgqsd->bqgrd', p_s.astype(vs.dtype), vs)
    s_w = jnp.einsum('bqgrd,bkgd->bgrqk', qg, k_w).astype(jnp.float32) * scale
    diff = q_pos[:, None] - w_pos[None, :]
    p_w = masked_softmax(s_w, (diff >= 0) & (diff <= WINDOW) & (w_pos[None, :] >= 0))
    o_w = jnp.einsum('bgrqk,bkgd->bqgrd', p_w.astype(v_w.dtype), v_w)
    g = gates.reshape(b, qb, NSA_KV_HEADS, NSA_GROUP, 3)
    o = g[..., 0:1] * o_c + g[..., 1:2] * o_s + g[..., 2:3] * o_w
    return o.reshape(b, qb, NSA_Q_W)


def sb_attend(q, q_pos, k_parts, v_parts, pos_parts):
    b, qb = q.shape[0], q.shape[1]
    scale = HEAD_DIM ** -0.5
    z = jnp.concatenate([jnp.einsum('bqhd,bkhd->bhqk', q, k) for k in k_parts], axis=-1).astype(jnp.float32) * scale
    k_pos = jnp.concatenate(pos_parts)
    mask = k_pos[None, :] < q_pos[:, None]
    log_keep = jnp.where(mask, jax.nn.log_sigmoid(-z), 0.0)
    later = lax.cumsum(log_keep, axis=z.ndim - 1, reverse=True) - log_keep
    a = jnp.where(mask, jnp.exp(jax.nn.log_sigmoid(z) + later), 0.0).astype(q.dtype)
    out = None
    start = 0
    for v in v_parts:
        n = v.shape[1]
        term = jnp.einsum('bhqk,bkhd->bqhd', a[..., start:start + n], v)
        out = term if out is None else out + term
        start += n
    return out.reshape(b, qb, SB_W)


def sweep_blocks(fn, length, block):
    starts = jnp.arange(length // block, dtype=jnp.int32) * block
    out = lax.map(fn, starts)
    return out.transpose(1, 0, 2, 3).reshape(out.shape[1], length, out.shape[3])


def merge(xn, o_nsa, o_sb, w_br_nsa, w_br_sb, w_gate, b_gate, w_out):
    g = jax.nn.sigmoid(xn @ w_gate + b_gate)
    g_nsa, g_sb = jnp.split(g, 2, axis=-1)
    return (g_nsa * (o_nsa @ w_br_nsa) + g_sb * (o_sb @ w_br_sb)) @ w_out


def mlp(x, w_up, w_down):
    return jnp.square(jax.nn.relu(x @ w_up)) @ w_down


def setup_inputs(seed: int = 0) -> dict:
    key = jax.random.key(seed)
    ks = jax.random.split(key, 24)
    n_pages = PAST_LEN // PAGE_SIZE
    n_used = DEC_BATCH * n_pages
    n_phys = n_used + n_used // 4
    win_len = min(WINDOW, PAST_LEN)
    nrm = lambda k, shape, s: jax.random.normal(k, shape, jnp.float32) * s
    perm = jax.random.permutation(ks[5], n_phys)
    page_table = perm[:n_used].reshape(DEC_BATCH, n_pages).astype(jnp.int32)
    return {
        "x_prompt": nrm(ks[0], (BATCH, SEQ, D_MODEL), 1.0),
        "x_sample": nrm(ks[1], (DEC_BATCH, DEC_SEQ, D_MODEL), 1.0),
        "cache_nsa": nrm(ks[2], (DEPTH, n_phys, PAGE_SIZE, 4, NSA_KV_HEADS, HEAD_DIM), 1.0),
        "cache_sb": nrm(ks[3], (DEPTH, n_phys, PAGE_SIZE, 2, SB_HEADS, HEAD_DIM), 1.0),
        "state_win": nrm(ks[4], (DEPTH, DEC_BATCH, win_len, 2, NSA_KV_HEADS, HEAD_DIM), 1.0),
        "page_table": page_table,
        "norm_mix": 1.0 + nrm(ks[6], (DEPTH, D_MODEL), 0.02),
        "norm_mlp": 1.0 + nrm(ks[7], (DEPTH, D_MODEL), 0.02),
        "w_in": nrm(ks[8], (DEPTH, D_MODEL, IN_COLS), D_MODEL ** -0.5),
        "q_norm": 1.0 + nrm(ks[9], (DEPTH, HEAD_DIM), 0.02),
        "k_norm": 1.0 + nrm(ks[10], (DEPTH, 3, HEAD_DIM), 0.02),
        "cmp_pos": nrm(ks[11], (DEPTH, 2, CMP_LEN, HEAD_DIM), 0.1),
        "cmp_w1": nrm(ks[12], (DEPTH, 2, CMP_LEN * HEAD_DIM, CMP_HIDDEN), (CMP_LEN * HEAD_DIM) ** -0.5),
        "cmp_b1": nrm(ks[13], (DEPTH, 2, CMP_HIDDEN), 0.01),
        "cmp_w2": nrm(ks[14], (DEPTH, 2, CMP_HIDDEN, HEAD_DIM), CMP_HIDDEN ** -0.5),
        "w_br_nsa": nrm(ks[15], (DEPTH, NSA_Q_W, D_MODEL), NSA_Q_W ** -0.5),
        "w_br_sb": nrm(ks[16], (DEPTH, SB_W, D_MODEL), SB_W ** -0.5),
        "w_gate": nrm(ks[17], (DEPTH, D_MODEL, 2 * D_MODEL), D_MODEL ** -0.5),
        "b_gate": nrm(ks[18], (DEPTH, 2 * D_MODEL), 0.01),
        "w_out": nrm(ks[19], (DEPTH, D_MODEL, D_MODEL), D_MODEL ** -0.5),
        "w_up": nrm(ks[20], (DEPTH, D_MODEL, D_FF), D_MODEL ** -0.5),
        "w_down": nrm(ks[21], (DEPTH, D_FF, D_MODEL), D_FF ** -0.5),
    }


def reference(x_prompt, x_sample, cache_nsa, cache_sb, state_win, page_table, norm_mix, norm_mlp, w_in, q_norm, k_norm, cmp_pos, cmp_w1, cmp_b1, cmp_w2, w_br_nsa, w_br_sb, w_gate, b_gate, w_out, w_up, w_down):
    seq = x_prompt.shape[1]
    db, ds = x_sample.shape[0], x_sample.shape[1]
    past_len = page_table.shape[1] * PAGE_SIZE
    win_len = state_win.shape[2]
    pos_p = jnp.arange(seq, dtype=jnp.int32)
    pos_s = past_len + jnp.arange(ds, dtype=jnp.int32)
    pos_past = jnp.arange(past_len, dtype=jnp.int32)
    xp, xs = x_prompt, x_sample
    nsa_p, sb_p, win_p, nsa_s, sb_s, win_s = [], [], [], [], [], []
    for l in range(DEPTH):
        xn = rmsnorm(xp, norm_mix[l])
        q, kc, vc, ks, vs, kw, vw, gates, sq, sk, sv = project(xn, pos_p, w_in[l], q_norm[l], k_norm[l])
        k_cb, v_cb, cb_end, k_sb, v_sb = nsa_keys(kc, vc, ks, vs, cmp_pos[l], cmp_w1[l], cmp_b1[l], cmp_w2[l])
        kw_pad = jnp.pad(kw, ((0, 0), (WINDOW, 0), (0, 0), (0, 0)))
        vw_pad = jnp.pad(vw, ((0, 0), (WINDOW, 0), (0, 0), (0, 0)))

        def nsa_body(t0):
            sl = lambda t, n: lax.dynamic_slice_in_dim(t, t0, n, axis=1)
            qpos = t0 + jnp.arange(NSA_Q_BLOCK, dtype=jnp.int32)
            wpos = t0 - WINDOW + jnp.arange(WINDOW + NSA_Q_BLOCK, dtype=jnp.int32)
            return nsa_block(sl(q, NSA_Q_BLOCK), qpos, sl(gates, NSA_Q_BLOCK), k_cb, v_cb, cb_end, k_sb, v_sb,
                             sl(kw_pad, WINDOW + NSA_Q_BLOCK), sl(vw_pad, WINDOW + NSA_Q_BLOCK), wpos)

        def sb_body(t0):
            qpos = t0 + jnp.arange(SB_Q_BLOCK, dtype=jnp.int32)
            return sb_attend(lax.dynamic_slice_in_dim(sq, t0, SB_Q_BLOCK, axis=1), qpos, (sk,), (sv,), (pos_p,))

        o_nsa = sweep_blocks(nsa_body, seq, NSA_Q_BLOCK)
        o_sb = sweep_blocks(sb_body, seq, SB_Q_BLOCK)
        xp = xp + merge(xn, o_nsa, o_sb, w_br_nsa[l], w_br_sb[l], w_gate[l], b_gate[l], w_out[l])
        xp = xp + mlp(rmsnorm(xp, norm_mlp[l]), w_up[l], w_down[l])
        nsa_p.append(jnp.stack([kc, vc, ks, vs], axis=2))
        sb_p.append(jnp.stack([sk, sv], axis=2))
        win_p.append(jnp.stack([kw, vw], axis=2)[:, seq - min(WINDOW, seq):])

        xn = rmsnorm(xs, norm_mix[l])
        q, kc, vc, ks, vs, kw, vw, gates, sq, sk, sv = project(xn, pos_s, w_in[l], q_norm[l], k_norm[l])
        past = cache_nsa[l, page_table].reshape(db, past_len, 4, NSA_KV_HEADS, HEAD_DIM)
        cat = lambda i, new: jnp.concatenate([past[:, :, i], new], axis=1)
        k_cb, v_cb, cb_end, k_sb, v_sb = nsa_keys(cat(0, kc), cat(1, vc), cat(2, ks), cat(3, vs),
                                                  cmp_pos[l], cmp_w1[l], cmp_b1[l], cmp_w2[l])
        win = state_win[l]
        kw_all = jnp.concatenate([win[:, :, 0], kw], axis=1)
        vw_all = jnp.concatenate([win[:, :, 1], vw], axis=1)
        wpos = past_len - win_len + jnp.arange(win_len + ds, dtype=jnp.int32)
        o_nsa = nsa_block(q, pos_s, gates, k_cb, v_cb, cb_end, k_sb, v_sb, kw_all, vw_all, wpos)
        sb_past = cache_sb[l, page_table].reshape(db, past_len, 2, SB_HEADS, HEAD_DIM)
        o_sb = sb_attend(sq, pos_s, (sb_past[:, :, 0], sk), (sb_past[:, :, 1], sv), (pos_past, pos_s))
        xs = xs + merge(xn, o_nsa, o_sb, w_br_nsa[l], w_br_sb[l], w_gate[l], b_gate[l], w_out[l])
        xs = xs + mlp(rmsnorm(xs, norm_mlp[l]), w_up[l], w_down[l])
        nsa_s.append(jnp.stack([kc, vc, ks, vs], axis=2))
        sb_s.append(jnp.stack([sk, sv], axis=2))
        keep = min(WINDOW, win_len + ds)
        win_s.append(jnp.stack([kw_all, vw_all], axis=2)[:, win_len + ds - keep:])
    new_nsa_prompt = jnp.stack(nsa_p)
    new_sb_prompt = jnp.stack(sb_p)
    new_win_prompt = jnp.stack(win_p)
    new_nsa_sample = jnp.stack(nsa_s)
    new_sb_sample = jnp.stack(sb_s)
    new_win_sample = jnp.stack(win_s)
    return (xp, xs, new_nsa_prompt, new_sb_prompt, new_win_prompt, new_nsa_sample, new_sb_sample, new_win_sample)
```

```python
import functools

import numpy as np
import jax
import jax.numpy as jnp
from jax import lax
from jax.experimental import pallas as pl
from jax.experimental.pallas import tpu as pltpu

F32 = jnp.float32
BF16 = jnp.bfloat16
I32 = jnp.int32

D_MODEL = 1024
HEAD_DIM = 64
NSA_HEADS = 8
NSA_KV_HEADS = 2
NSA_GROUP = NSA_HEADS // NSA_KV_HEADS
CMP_STRIDE = 16
CMP_LEN = 32
CMP_HIDDEN = 128
SLC_BLOCK = 64
SLC_TOP_N = 16
WINDOW = 512
SB_HEADS = 8
D_FF = 4 * D_MODEL
ROPE_THETA = 10000.0
RMS_EPS = 1e-6
FORCE_BONUS = 1e4
PAGE_SIZE = 128
SCALE = HEAD_DIM ** -0.5

LANES = 128
SUBLANES = 8
VMEM_LIMIT_BYTES = 56 * 1024 * 1024

NEG = -1e30
EXP_ZERO_BELOW = -88.0

NSA_Q_W = NSA_HEADS * HEAD_DIM
NSA_KV_W = NSA_KV_HEADS * HEAD_DIM
SB_W = SB_HEADS * HEAD_DIM
ROPED_W = NSA_Q_W + 3 * NSA_KV_W
PLAIN_W = 3 * NSA_KV_W + 3 * SB_W
Q_TILE = SLC_BLOCK
SB_TILE = 128
SEL_CHUNK = 512
WIN_KEYS = WINDOW + 2 * Q_TILE
CMP_PAGES = 16


def _dot(a, b):
    return jnp.dot(a, b, preferred_element_type=F32)


def _dot_nt(a, b):
    return lax.dot_general(a, b, (((1,), (1,)), ((), ())), preferred_element_type=F32)


def _split2(x):
    hi = x.astype(BF16)
    lo = (x - hi.astype(F32)).astype(BF16)
    return hi, lo


def _split3(x):
    hi = x.astype(BF16)
    r = x - hi.astype(F32)
    mid = r.astype(BF16)
    lo = (r - mid.astype(F32)).astype(BF16)
    return hi, mid, lo


def _rmsnorm(x, gain):
    ms = jnp.mean(x * x, axis=-1, keepdims=True)
    return x * lax.rsqrt(ms + RMS_EPS) * gain


def _softplus(z):
    return jnp.maximum(z, 0.0) + jnp.log(1.0 + jnp.exp(-jnp.abs(z)))


def _masked_softmax_parts(s, mask):
    sm = jnp.where(mask, s, NEG)
    m = jnp.max(sm, axis=-1, keepdims=True)
    p = jnp.where(mask, jnp.exp(sm - m), 0.0)
    den = jnp.maximum(jnp.sum(p, axis=-1, keepdims=True), 1e-30)
    return p, den


def _params(semantics):
    return pltpu.CompilerParams(dimension_semantics=semantics, vmem_limit_bytes=VMEM_LIMIT_BYTES)


def _const_spec(shape):
    n = len(shape)
    return pl.BlockSpec(shape, lambda *_: (0,) * n)


def _proj_kernel(x_ref, gmix_ref, wa_ref, wb_ref, wg_ref, ga_ref, cos_ref, sin_ref, seg_ref,
                 nsa_ref, sbn_ref, win_ref, q_ref, nsab_ref, sbb_ref, gate_ref):
    xn = _rmsnorm(x_ref[...], gmix_ref[...]).astype(BF16)
    ha = _dot(xn, wa_ref[...])
    hb = _dot(xn, wb_ref[...])
    gate_ref[...] = jax.nn.sigmoid(_dot(xn, wg_ref[...]))
    cos = cos_ref[...]
    sin = sin_ref[...]
    seg = seg_ref[...]
    lane = lax.broadcasted_iota(I32, (1, LANES), 1)
    first_half = (lane & (HEAD_DIM - 1)) < (HEAD_DIM // 2)
    roped = []
    for j in range(ROPED_W // LANES):
        h = ha[:, j * LANES:(j + 1) * LANES]
        hi, lo = _split2(h * h)
        ms = _dot(hi, seg) + _dot(lo, seg)
        y = h * lax.rsqrt(ms + RMS_EPS) * ga_ref[:, j * LANES:(j + 1) * LANES]
        rot = jnp.where(first_half, pltpu.roll(y, LANES - HEAD_DIM // 2, 1), pltpu.roll(y, HEAD_DIM // 2, 1))
        roped.append(y * cos + rot * sin)
    nq = NSA_Q_W // LANES
    for j in range(nq):
        q_ref[:, j * LANES:(j + 1) * LANES] = (roped[j] * SCALE).astype(BF16)
    kc, ks, kw = roped[nq], roped[nq + 1], roped[nq + 2]
    vc, vs, vw = hb[:, 0:128], hb[:, 128:256], hb[:, 256:384]
    sq, sk, sv = hb[:, 384:896], hb[:, 896:1408], hb[:, 1408:1920]
    nsa_ref[:, 0:128] = kc
    nsa_ref[:, 128:256] = vc
    nsa_ref[:, 256:384] = ks
    nsa_ref[:, 384:512] = vs
    sbn_ref[:, 0:512] = sk
    sbn_ref[:, 512:1024] = sv
    win_ref[:, 0:128] = kw
    win_ref[:, 128:256] = vw
    nsab_ref[:, 0:128] = ks.astype(BF16)
    nsab_ref[:, 128:256] = vs.astype(BF16)
    nsab_ref[:, 256:384] = kw.astype(BF16)
    nsab_ref[:, 384:512] = vw.astype(BF16)
    sbb_ref[:, 0:512] = (sq * SCALE).astype(BF16)
    sbb_ref[:, 512:1024] = sk.astype(BF16)
    sbb_ref[:, 1024:1536] = sv.astype(BF16)


def _proj(x2, lw, cos, sin, *, tm, pos_blocks):
    m = x2.shape[0]
    row = lambda w: pl.BlockSpec((tm, w), lambda i: (i, 0))
    pos = pl.BlockSpec((tm, LANES), lambda i: (i % pos_blocks, 0))
    in_specs = [row(D_MODEL), _const_spec((1, D_MODEL)), _const_spec((D_MODEL, ROPED_W)),
                _const_spec((D_MODEL, PLAIN_W)), _const_spec((D_MODEL, LANES)), _const_spec((1, ROPED_W)),
                pos, pos, _const_spec((LANES, LANES))]
    widths = [(512, F32), (1024, F32), (256, F32), (512, BF16), (512, BF16), (1536, BF16), (128, F32)]
    return pl.pallas_call(
        _proj_kernel,
        grid=(m // tm,),
        in_specs=in_specs,
        out_specs=[row(w) for w, _ in widths],
        out_shape=[jax.ShapeDtypeStruct((m, w), dt) for w, dt in widths],
        compiler_params=_params(("parallel",)),
    )(x2, lw["gmix"], lw["wa"], lw["wb"], lw["wg"], lw["ga"], cos, sin, lw["seg"])


def _cmp_mlp(load_rows, slot, posa_ref, posb_ref, w1a_ref, w1b_ref, b1_ref, w2_ref):
    acc_a = acc_b = None
    for r in range(CMP_STRIDE):
        xr = load_rows(r)
        pa = _dot((xr + posa_ref[slot, r]).astype(BF16), w1a_ref[slot, r])
        pb = _dot((xr + posb_ref[slot, r]).astype(BF16), w1b_ref[slot, r])
        acc_a = pa if acc_a is None else acc_a + pa
        acc_b = pb if acc_b is None else acc_b + pb
    n = acc_a.shape[0]
    h = acc_a + pltpu.roll(acc_b, n - 1, 0) + b1_ref[slot]
    h = h * jax.nn.sigmoid(h)
    return _dot(h.astype(BF16), w2_ref[slot])


def _cmp_prompt_kernel(xk_ref, xv_ref, posa_ref, posb_ref, w1a_ref, w1b_ref, b1_ref, w2_ref, o_ref):
    nc = o_ref.shape[0]
    for slot, x_ref in enumerate((xk_ref, xv_ref)):
        load = lambda r, x_ref=x_ref: x_ref[pl.ds(r, nc, stride=CMP_STRIDE), :]
        out = _cmp_mlp(load, slot, posa_ref, posb_ref, w1a_ref, w1b_ref, b1_ref, w2_ref)
        o_ref[:, slot * LANES:(slot + 1) * LANES] = out.astype(BF16)


def _cmp_weight_specs():
    return [_const_spec((2, CMP_STRIDE, 1, LANES)), _const_spec((2, CMP_STRIDE, 1, LANES)),
            _const_spec((2, CMP_STRIDE, LANES, 256)), _const_spec((2, CMP_STRIDE, LANES, 256)),
            _const_spec((2, 1, 256)), _const_spec((2, 256, LANES))]


def _cmp_weights(lw):
    return (lw["posa"], lw["posb"], lw["w1a"], lw["w1b"], lw["b1"], lw["w2"])


def _cmp_prompt(nsa_new, lw, *, batch, seq):
    nc = seq // CMP_STRIDE
    return pl.pallas_call(
        _cmp_prompt_kernel,
        grid=(batch,),
        in_specs=[pl.BlockSpec((seq, LANES), lambda b: (b, 0)), pl.BlockSpec((seq, LANES), lambda b: (b, 1))]
        + _cmp_weight_specs(),
        out_specs=pl.BlockSpec((None, nc, 256), lambda b: (b, 0, 0)),
        out_shape=jax.ShapeDtypeStruct((batch, nc, 256), BF16),
        compiler_params=_params(("parallel",)),
    )(nsa_new, nsa_new, *_cmp_weights(lw))


def _cmp_sample_kernel(pt_ref, cache_ref, posa_ref, posb_ref, w1a_ref, w1b_ref, b1_ref, w2_ref, o_ref, buf, sem,
                       *, layer, pages, n_pages, steps, total):
    i = pl.program_id(0)
    per_page = PAGE_SIZE // CMP_STRIDE

    def copies(j, half):
        b = j // steps
        s = j % steps
        out = []
        for slot in range(2):
            for k in range(pages + 1):
                p = jnp.minimum(s * pages + k, n_pages - 1)
                src = cache_ref.at[layer, pt_ref[b * n_pages + p], :, pl.ds(slot * LANES, LANES)]
                out.append(pltpu.make_async_copy(src, buf.at[half, slot, k], sem.at[half]))
        return out

    @pl.when(i == 0)
    def _():
        for c in copies(i, 0):
            c.start()

    @pl.when(i + 1 < total)
    def _():
        for c in copies(i + 1, (i + 1) & 1):
            c.start()

    half = i & 1
    for c in copies(i, half):
        c.wait()
    for slot in range(2):
        def load(r, slot=slot):
            rows = [buf[half, slot, k, pl.ds(r, per_page, stride=CMP_STRIDE), :] for k in range(pages + 1)]
            return jnp.concatenate(rows, axis=0)

        out = _cmp_mlp(load, slot, posa_ref, posb_ref, w1a_ref, w1b_ref, b1_ref, w2_ref)
        o_ref[:, slot * LANES:(slot + 1) * LANES] = out[:pages * per_page].astype(BF16)


def _cmp_sample(cache4, pt_flat, lw, *, layer, dec_batch, n_pages):
    pages = min(CMP_PAGES, n_pages)
    steps = n_pages // pages
    per_page = PAGE_SIZE // CMP_STRIDE
    nc = n_pages * per_page
    total = dec_batch * steps
    wspecs = [pl.BlockSpec(sp.block_shape, lambda i, pt, n=len(sp.block_shape): (0,) * n)
              for sp in _cmp_weight_specs()]
    grid_spec = pltpu.PrefetchScalarGridSpec(
        num_scalar_prefetch=1,
        grid=(total,),
        in_specs=[pl.BlockSpec(memory_space=pl.ANY)] + wspecs,
        out_specs=pl.BlockSpec((None, pages * per_page, 256), lambda i, pt: (i // steps, i % steps, 0)),
        scratch_shapes=[pltpu.VMEM((2, 2, pages + 1, PAGE_SIZE, LANES), F32), pltpu.SemaphoreType.DMA((2,))],
    )
    kern = functools.partial(_cmp_sample_kernel, layer=layer, pages=pages, n_pages=n_pages, steps=steps, total=total)
    return pl.pallas_call(
        kern,
        grid_spec=grid_spec,
        out_shape=jax.ShapeDtypeStruct((dec_batch, nc, 256), BF16),
        compiler_params=_params(("arbitrary",)),
    )(pt_flat, cache4, *_cmp_weights(lw))


def _beats(bi, key, srow, i):
    tie = jnp.where(srow > i, 1.0, 0.0)
    return jnp.where(bi > key, 1.0, jnp.where(bi == key, tie, 0.0))


def _rank_static(key, srow):
    cnt = jnp.zeros(key.shape, F32)
    for i in range(key.shape[0]):
        cnt = cnt + _beats(key[i:i + 1, :], key, srow, i)
    return cnt


def _forced(srow, cur):
    return (srow == 0) | (srow == cur) | (srow == cur - 1)


def _nsa_prompt_kernel(q_ref, gate_ref, cb_ref, kv_ref, amat_ref, e3_ref, eye_ref, o_ref,
                       *, seq, n_sel, n_top):
    t = pl.program_id(1)
    t0 = t * Q_TILE
    ncb = cb_ref.shape[0]
    rows = NSA_GROUP * Q_TILE
    lane = lax.broadcasted_iota(I32, (1, LANES), 1)
    lane_group = lane >> 6
    qpos = t0 + lax.broadcasted_iota(I32, (Q_TILE, 1), 0)
    qpos4 = jnp.concatenate([qpos] * NSA_GROUP, axis=0)
    q = q_ref[...].astype(F32)
    gates = gate_ref[...]

    def q_slab(g):
        parts = []
        for r in range(NSA_GROUP):
            h = NSA_GROUP * g + r
            blk = q[:, (h // 2) * LANES:(h // 2 + 1) * LANES]
            if h % 2 != g:
                blk = pltpu.roll(blk, HEAD_DIM, 1)
            parts.append(jnp.where(lane_group == g, blk, 0.0))
        return jnp.concatenate(parts, axis=0).astype(BF16)

    qs = [q_slab(g) for g in range(NSA_KV_HEADS)]

    kcb = cb_ref[:, 0:LANES]
    vcb = cb_ref[:, LANES:2 * LANES]
    cb_end = lax.broadcasted_iota(I32, (1, ncb), 1) * CMP_STRIDE + (CMP_LEN - 1)
    cmask = cb_end <= qpos4
    o_cmp, imps = [], []
    for g in range(NSA_KV_HEADS):
        p, den = _masked_softmax_parts(_dot_nt(qs[g], kcb), cmask)
        p = p / den
        o_cmp.append(_dot(p.astype(BF16), vcb))
        imps.append(p[0:Q_TILE] + p[Q_TILE:2 * Q_TILE] + p[2 * Q_TILE:3 * Q_TILE] + p[3 * Q_TILE:4 * Q_TILE])
    imp = jnp.concatenate(imps, axis=0)
    amat = amat_ref[...]
    imp_s = sum(_dot_nt(amat, part) for part in _split3(imp))
    srow = lax.broadcasted_iota(I32, (n_sel, 1), 0)
    qpos_lane = t0 + (lane & (Q_TILE - 1))
    valid = srow * SLC_BLOCK <= qpos_lane
    bonus = jnp.where(_forced(srow, t), FORCE_BONUS, 0.0)
    key = jnp.where(valid, imp_s + bonus, -jnp.inf)
    sel_t = jnp.where(_rank_static(key, srow) < n_top, 1.0, 0.0).astype(BF16)
    sel_q = _dot_nt(eye_ref[...], sel_t).astype(BF16)

    ck = e3_ref.shape[2]
    kcol = lax.broadcasted_iota(I32, (1, ck), 1)
    n_chunks = t // (ck // Q_TILE) + 1

    def sel_branch(g):
        selg = sel_q[g * Q_TILE:(g + 1) * Q_TILE, :]

        def body(c, carry):
            m, l, acc = carry
            k0 = pl.multiple_of(c * ck, ck)
            kch = kv_ref[pl.ds(k0, ck), 0:LANES]
            vch = kv_ref[pl.ds(k0, ck), LANES:2 * LANES]
            s = _dot_nt(qs[g], kch)
            allow = (_dot(selg, e3_ref[c]) > 0.5) & ((k0 + kcol) <= qpos)
            allow4 = jnp.concatenate([allow] * NSA_GROUP, axis=0)
            s = jnp.where(allow4, s, NEG)
            m_new = jnp.maximum(m, jnp.max(s, axis=-1, keepdims=True))
            alpha = jnp.exp(m - m_new)
            p = jnp.exp(s - m_new)
            l = alpha * l + jnp.sum(p, axis=-1, keepdims=True)
            acc = alpha * acc + _dot(p.astype(BF16), vch)
            return m_new, l, acc

        init = (jnp.full((rows, 1), NEG, F32), jnp.zeros((rows, 1), F32), jnp.zeros((rows, LANES), F32))
        _, l, acc = lax.fori_loop(0, n_chunks, body, init)
        return acc / l

    wk = min(WIN_KEYS, seq)
    start = pl.multiple_of(jnp.clip(t0 - WINDOW, 0, seq - wk), Q_TILE)
    kwin = kv_ref[pl.ds(start, wk), 2 * LANES:3 * LANES]
    vwin = kv_ref[pl.ds(start, wk), 3 * LANES:4 * LANES]
    diff = qpos4 - (start + lax.broadcasted_iota(I32, (1, wk), 1))
    wmask = (diff >= 0) & (diff <= WINDOW)

    def win_branch(g):
        p, den = _masked_softmax_parts(_dot_nt(qs[g], kwin), wmask)
        return _dot(p.astype(BF16), vwin) / den

    for g in range(NSA_KV_HEADS):
        o_sel = sel_branch(g)
        o_win = win_branch(g)
        heads = []
        for r in range(NSA_GROUP):
            h = NSA_GROUP * g + r
            sl = slice(r * Q_TILE, (r + 1) * Q_TILE)
            gc = [gates[:, 3 * h + c:3 * h + c + 1] for c in range(3)]
            oh = gc[0] * o_cmp[g][sl] + gc[1] * o_sel[sl] + gc[2] * o_win[sl]
            if h % 2 != g:
                oh = pltpu.roll(oh, HEAD_DIM, 1)
            heads.append(oh)
        for pr in range(NSA_GROUP // 2):
            hp = (NSA_GROUP * g) // 2 + pr
            pair = jnp.where(lane_group == 0, heads[2 * pr], heads[2 * pr + 1])
            o_ref[:, hp * LANES:(hp + 1) * LANES] = pair.astype(BF16)


def _nsa_prompt(q, gates, cb, nsab, consts, *, batch, seq):
    nt = seq // Q_TILE
    ncb = seq // CMP_STRIDE
    n_sel = seq // SLC_BLOCK
    ck = min(SEL_CHUNK, seq)
    kern = functools.partial(_nsa_prompt_kernel, seq=seq, n_sel=n_sel, n_top=min(SLC_TOP_N, n_sel))
    tile = lambda w: pl.BlockSpec((Q_TILE, w), lambda b, t: (b * nt + t, 0))
    return pl.pallas_call(
        kern,
        grid=(batch, nt),
        in_specs=[tile(NSA_Q_W), tile(LANES),
                  pl.BlockSpec((None, ncb, 256), lambda b, t: (b, 0, 0)),
                  pl.BlockSpec((seq, 512), lambda b, t: (b, 0)),
                  _const_spec((n_sel, ncb)), _const_spec((seq // ck, n_sel, ck)), _const_spec((LANES, LANES))],
        out_specs=tile(NSA_Q_W),
        out_shape=jax.ShapeDtypeStruct((batch * seq, NSA_Q_W), BF16),
        compiler_params=_params(("parallel", "arbitrary")),
    )(q, gates, cb, nsab, consts["amat_p"], consts["e3"], consts["eye"])


def _sb_block(qh, k, v, tri, carry, acc, mask):
    z = _dot_nt(qh, k)
    sp = _softplus(z)
    lk = -sp if mask is None else jnp.where(mask, -sp, 0.0)
    hi, lo = _split2(lk)
    later = _dot(hi, tri) + _dot(lo, tri)
    ones = jnp.ones((lk.shape[1], LANES), BF16)
    total = _dot(hi, ones) + _dot(lo, ones)
    a = jnp.exp(z - sp + later + carry)
    if mask is not None:
        a = jnp.where(mask, a, 0.0)
    return carry + total, acc + _dot(a.astype(BF16), v)


def _sb_prompt_kernel(q_ref, k_ref, v_ref, tri_ref, o_ref):
    qi = pl.program_id(2)
    tq = SB_TILE
    q = q_ref[...]
    tri = tri_ref[...]
    lane = lax.broadcasted_iota(I32, (1, LANES), 1)
    row = lax.broadcasted_iota(I32, (tq, 1), 0)
    col = lax.broadcasted_iota(I32, (1, tq), 1)
    diag_mask = col < row
    outs = []
    for hh in range(2):
        qh = jnp.where((lane >> 6) == hh, q, jnp.zeros_like(q))

        def load(kb):
            k0 = pl.multiple_of(kb * tq, tq)
            return k_ref[pl.ds(k0, tq), :], v_ref[pl.ds(k0, tq), :]

        k, v = load(qi)
        zeros = jnp.zeros((tq, LANES), F32)
        carry, acc = _sb_block(qh, k, v, tri, zeros, zeros, diag_mask)

        def cond(st):
            kb, go, _, _ = st
            return (kb >= 0) & (go > 0)

        def body(st):
            kb, _, carry, acc = st
            k, v = load(kb)
            carry, acc = _sb_block(qh, k, v, tri, carry, acc, None)
            go = (jnp.max(carry) > EXP_ZERO_BELOW).astype(I32)
            return kb - 1, go, carry, acc

        go0 = (jnp.max(carry) > EXP_ZERO_BELOW).astype(I32)
        _, _, _, acc = lax.while_loop(cond, body, (qi - 1, go0, carry, acc))
        outs.append(acc)
    o_ref[...] = jnp.where((lane >> 6) == 0, outs[0], outs[1]).astype(BF16)


def _sb_prompt(sbb, consts, *, batch, seq):
    nq = seq // SB_TILE
    pairs = SB_W // LANES
    return pl.pallas_call(
        _sb_prompt_kernel,
        grid=(batch, pairs, nq),
        in_specs=[pl.BlockSpec((SB_TILE, LANES), lambda b, hp, qi: (b * nq + qi, hp)),
                  pl.BlockSpec((seq, LANES), lambda b, hp, qi: (b, pairs + hp)),
                  pl.BlockSpec((seq, LANES), lambda b, hp, qi: (b, 2 * pairs + hp)),
                  _const_spec((SB_TILE, SB_TILE))],
        out_specs=pl.BlockSpec((SB_TILE, LANES), lambda b, hp, qi: (b * nq + qi, hp)),
        out_shape=jax.ShapeDtypeStruct((batch * seq, SB_W), BF16),
        compiler_params=_params(("parallel", "parallel", "arbitrary")),
    )(sbb, sbb, sbb, consts["tri"])


def _merge_kernel(x_ref, on_ref, os_ref, gmix_ref, wgate_ref, bgate_ref, wbn_ref, wbs_ref, wout_ref, o_ref):
    x = x_ref[...]
    xn = _rmsnorm(x, gmix_ref[...]).astype(BF16)
    g = jax.nn.sigmoid(_dot(xn, wgate_ref[...]) + bgate_ref[...])
    u = g[:, :D_MODEL] * _dot(on_ref[...], wbn_ref[...]) + g[:, D_MODEL:] * _dot(os_ref[...], wbs_ref[...])
    o_ref[...] = x + _dot(u.astype(BF16), wout_ref[...])


def _mlp_kernel(x_ref, gmlp_ref, wup_ref, wdn_ref, o_ref):
    x = x_ref[...]
    h = _rmsnorm(x, gmlp_ref[...]).astype(BF16)
    acc = None
    step = 1024
    for c in range(D_FF // step):
        up = _dot(h, wup_ref[:, c * step:(c + 1) * step])
        up = jnp.square(jnp.maximum(up, 0.0)).astype(BF16)
        part = _dot(up, wdn_ref[c * step:(c + 1) * step, :])
        acc = part if acc is None else acc + part
    o_ref[...] = x + acc


def _resident(shape):
    n = len(shape)
    return pl.BlockSpec(shape, lambda *_: (0,) * n, pipeline_mode=pl.Buffered(1))


def _merge(x2, o_nsa, o_sb, lw, *, tm):
    m = x2.shape[0]
    row = lambda w: pl.BlockSpec((tm, w), lambda i: (i, 0))
    return pl.pallas_call(
        _merge_kernel,
        grid=(m // tm,),
        in_specs=[row(D_MODEL), row(NSA_Q_W), row(SB_W), _resident((1, D_MODEL)),
                  _resident((D_MODEL, 2 * D_MODEL)), _resident((1, 2 * D_MODEL)),
                  _resident((NSA_Q_W, D_MODEL)), _resident((SB_W, D_MODEL)), _resident((D_MODEL, D_MODEL))],
        out_specs=row(D_MODEL),
        out_shape=jax.ShapeDtypeStruct((m, D_MODEL), F32),
        compiler_params=_params(("parallel",)),
    )(x2, o_nsa, o_sb, lw["gmix"], lw["wgate"], lw["bgate"], lw["wbn"], lw["wbs"], lw["wout"])


def _mlp(x2, lw, *, tm):
    m = x2.shape[0]
    row = pl.BlockSpec((tm, D_MODEL), lambda i: (i, 0))
    return pl.pallas_call(
        _mlp_kernel,
        grid=(m // tm,),
        in_specs=[row, _resident((1, D_MODEL)), _resident((D_MODEL, D_FF)), _resident((D_FF, D_MODEL))],
        out_specs=row,
        out_shape=jax.ShapeDtypeStruct((m, D_MODEL), F32),
        compiler_params=_params(("parallel",)),
    )(x2, lw["gmlp"], lw["wup"], lw["wdn"])


def _nsa_sample_cmp_kernel(qz_ref, cb_ref, amat_ref, oc_ref, imps_ref, *, q_pos):
    qz = qz_ref[...]
    nc = cb_ref.shape[0]
    kcb = cb_ref[:, 0:LANES]
    vcb = cb_ref[:, LANES:2 * LANES]
    cb_end = lax.broadcasted_iota(I32, (1, nc), 1) * CMP_STRIDE + (CMP_LEN - 1)
    p, den = _masked_softmax_parts(_dot_nt(qz, kcb), cb_end <= q_pos)
    p = p / den
    oc_ref[...] = _dot(p.astype(BF16), vcb)
    row = lax.broadcasted_iota(I32, (NSA_HEADS, 1), 0)
    g0 = jnp.sum(p[0:NSA_GROUP], axis=0, keepdims=True)
    g1 = jnp.sum(p[NSA_GROUP:], axis=0, keepdims=True)
    imp = jnp.where(row < NSA_GROUP, g0, g1)
    amat = amat_ref[...]
    imps_ref[...] = sum(_dot(part, amat) for part in _split3(imp))


def _nsa_sample_cmp(qz, cb, consts, *, dec_batch, q_pos):
    nc = cb.shape[1]
    ns_pad = consts["amat_s"].shape[1]
    return pl.pallas_call(
        functools.partial(_nsa_sample_cmp_kernel, q_pos=q_pos),
        grid=(dec_batch,),
        in_specs=[pl.BlockSpec((None, NSA_HEADS, LANES), lambda b: (b, 0, 0)),
                  pl.BlockSpec((None, nc, 256), lambda b: (b, 0, 0)),
                  _const_spec((nc, ns_pad))],
        out_specs=[pl.BlockSpec((None, NSA_HEADS, LANES), lambda b: (b, 0, 0)),
                   pl.BlockSpec((None, NSA_HEADS, ns_pad), lambda b: (b, 0, 0))],
        out_shape=[jax.ShapeDtypeStruct((dec_batch, NSA_HEADS, LANES), F32),
                   jax.ShapeDtypeStruct((dec_batch, NSA_HEADS, ns_pad), F32)],
        compiler_params=_params(("parallel",)),
    )(qz, cb, consts["amat_s"])


def _topk_sample_kernel(imp_ref, idx_ref, key_ref, *, n_sel, n_top):
    nrows = imp_ref.shape[0]
    srow = lax.broadcasted_iota(I32, (nrows, 1), 0)
    bonus = jnp.where(_forced(srow, n_sel - 1), FORCE_BONUS, 0.0)
    key = jnp.where(srow < n_sel, imp_ref[...] + bonus, -jnp.inf)
    key_ref[...] = key

    def body(i, cnt):
        return cnt + _beats(key_ref[pl.ds(i, 1), :], key, srow, i)

    cnt = lax.fori_loop(0, n_sel, body, jnp.zeros(key.shape, F32))
    srow_f = srow.astype(F32)
    picks = [jnp.sum(jnp.where(cnt == float(k), srow_f, 0.0), axis=0, keepdims=True) for k in range(n_top)]
    idx_ref[...] = jnp.concatenate(picks, axis=0).astype(I32)


def _topk_sample(imp_t, *, n_sel, n_top):
    nrows = imp_t.shape[0]
    return pl.pallas_call(
        functools.partial(_topk_sample_kernel, n_sel=n_sel, n_top=n_top),
        out_shape=jax.ShapeDtypeStruct((n_top, LANES), I32),
        scratch_shapes=[pltpu.VMEM((nrows, LANES), F32)],
    )(imp_t)


def _nsa_sample_sel_kernel(idx_ref, pt_ref, qz_ref, own_ref, wown_ref, win_ref, oc_ref, gate_ref, cache_ref,
                           o_ref, buf, sem, *, layer, n_pages, n_top):
    b = pl.program_id(0)
    n_past_blk = n_pages * (PAGE_SIZE // SLC_BLOCK)
    per_page = PAGE_SIZE // SLC_BLOCK

    def blk_of(g, k):
        return idx_ref[(b * NSA_KV_HEADS + g) * n_top + k]

    def copy(g, k):
        blk = jnp.minimum(blk_of(g, k), n_past_blk - 1)
        page = pt_ref[b * n_pages + blk // per_page]
        off = pl.multiple_of((blk % per_page) * SLC_BLOCK, SLC_BLOCK)
        src = cache_ref.at[layer, page, pl.ds(off, SLC_BLOCK), pl.ds(2 * LANES, 2 * LANES)]
        return pltpu.make_async_copy(src, buf.at[g, k], sem.at[g, k])

    for g in range(NSA_KV_HEADS):
        for k in range(n_top):
            copy(g, k).start()

    qz = qz_ref[...]
    qf = qz.astype(F32)
    row = lax.broadcasted_iota(I32, (NSA_HEADS, 1), 0)
    top_rows = row < NSA_GROUP
    n_keys = n_top * SLC_BLOCK
    kslot = lax.broadcasted_iota(I32, (1, n_keys), 1) >> 6

    for g in range(NSA_KV_HEADS):
        for k in range(n_top):
            copy(g, k).wait()

    s_parts, m_parts, own_parts, v_parts = [], [], [], []
    for g in range(NSA_KV_HEADS):
        kg = buf[g, :, :, 0:LANES].reshape(n_keys, LANES).astype(BF16)
        v_parts.append(buf[g, :, :, LANES:2 * LANES].reshape(n_keys, LANES).astype(BF16))
        s_parts.append(_dot_nt(qz, kg))
        msk = jnp.zeros((1, n_keys), I32)
        has_own = jnp.int32(0)
        for k in range(n_top):
            in_past = (blk_of(g, k) < n_past_blk).astype(I32)
            msk = jnp.where(kslot == k, in_past, msk)
            has_own = jnp.maximum(has_own, 1 - in_past)
        m_parts.append(msk)
        own_parts.append(has_own)
    s = jnp.where(top_rows, s_parts[0], s_parts[1])
    mask = jnp.where(top_rows, m_parts[0], m_parts[1]) > 0
    own_ok = jnp.where(top_rows, own_parts[0], own_parts[1]) > 0
    own = own_ref[...]
    s_own = jnp.sum(qf * own[:, 2 * LANES:3 * LANES], axis=-1, keepdims=True)
    sm = jnp.where(mask, s, NEG)
    m = jnp.maximum(jnp.max(sm, axis=-1, keepdims=True), jnp.where(own_ok, s_own, NEG))
    p = jnp.where(mask, jnp.exp(sm - m), 0.0)
    p_own = jnp.where(own_ok, jnp.exp(s_own - m), 0.0)
    den = jnp.maximum(jnp.sum(p, axis=-1, keepdims=True) + p_own, 1e-30)
    pb = p.astype(BF16)
    o_sel = jnp.where(top_rows, _dot(pb, v_parts[0]), _dot(pb, v_parts[1]))
    o_sel = (o_sel + p_own * own[:, 3 * LANES:4 * LANES]) / den

    wown = wown_ref[...]
    kwin = win_ref[:, 0:LANES].astype(BF16)
    vwin = win_ref[:, LANES:2 * LANES].astype(BF16)
    sw = _dot_nt(qz, kwin)
    sw_own = jnp.sum(qf * wown[:, 0:LANES], axis=-1, keepdims=True)
    mw = jnp.maximum(jnp.max(sw, axis=-1, keepdims=True), sw_own)
    pw = jnp.exp(sw - mw)
    pw_own = jnp.exp(sw_own - mw)
    denw = jnp.sum(pw, axis=-1, keepdims=True) + pw_own
    o_win = (_dot(pw.astype(BF16), vwin) + pw_own * wown[:, LANES:2 * LANES]) / denw

    gates = gate_ref[...]
    o_ref[...] = gates[:, 0:1] * oc_ref[...] + gates[:, 1:2] * o_sel + gates[:, 2:3] * o_win


def _nsa_sample_sel(idx_flat, pt_flat, qz, nsa_new, win_new, win4, o_cmp, gates8, cache4,
                    *, layer, dec_batch, n_pages, n_top):
    win_len = win4.shape[2]
    per_b = lambda shape: pl.BlockSpec((None,) + shape, lambda b, *_: (b,) + (0,) * len(shape))
    grid_spec = pltpu.PrefetchScalarGridSpec(
        num_scalar_prefetch=2,
        grid=(dec_batch,),
        in_specs=[per_b((NSA_HEADS, LANES)), per_b((1, 512)), per_b((1, 256)),
                  pl.BlockSpec((None, None, win_len, 256), lambda b, *_: (layer, b, 0, 0)),
                  per_b((NSA_HEADS, LANES)), per_b((NSA_HEADS, LANES)),
                  pl.BlockSpec(memory_space=pl.ANY)],
        out_specs=per_b((NSA_HEADS, LANES)),
        scratch_shapes=[pltpu.VMEM((NSA_KV_HEADS, n_top, SLC_BLOCK, 2 * LANES), F32),
                        pltpu.SemaphoreType.DMA((NSA_KV_HEADS, n_top))],
    )
    return pl.pallas_call(
        functools.partial(_nsa_sample_sel_kernel, layer=layer, n_pages=n_pages, n_top=n_top),
        grid_spec=grid_spec,
        out_shape=jax.ShapeDtypeStruct((dec_batch, NSA_HEADS, LANES), F32),
        compiler_params=_params(("arbitrary",)),
    )(idx_flat, pt_flat, qz, nsa_new.reshape(dec_batch, 1, 512), win_new.reshape(dec_batch, 1, 256), win4,
      o_cmp, gates8, cache4)


def _sb_sample_kernel(pt_ref, qbd_ref, tri_ref, bd_ref, cache_ref, o_ref, buf, sem, *, layer, n_pages):
    b = pl.program_id(0)

    def copy(p, slot):
        return pltpu.make_async_copy(cache_ref.at[layer, pt_ref[b * n_pages + p]], buf.at[slot], sem.at[slot])

    copy(n_pages - 1, 0).start()
    qbd = qbd_ref[...]
    tri = tri_ref[...]

    def cond(st):
        p, _, go, _, _ = st
        return (p >= 0) & (go > 0)

    def body(st):
        p, it, _, carry, acc = st
        slot = it & 1

        @pl.when(p >= 1)
        def _():
            copy(p - 1, 1 - slot).start()

        copy(p, slot).wait()
        k = buf[slot, :, 0:SB_W].astype(BF16)
        v = buf[slot, :, SB_W:2 * SB_W].astype(BF16)
        carry, acc = _sb_block(qbd, k, v, tri, carry, acc, None)
        go = (jnp.max(carry) > EXP_ZERO_BELOW).astype(I32)
        return p - 1, it + 1, go, carry, acc

    init = (jnp.int32(n_pages - 1), jnp.int32(0), jnp.int32(1),
            jnp.zeros((SB_HEADS, LANES), F32), jnp.zeros((SB_HEADS, SB_W), F32))
    p_end, it_end, _, _, acc = lax.while_loop(cond, body, init)

    @pl.when(p_end >= 0)
    def _():
        copy(p_end, it_end & 1).wait()

    o_ref[...] = jnp.sum(acc * bd_ref[...], axis=0, keepdims=True)


def _sb_sample(pt_flat, qbd, cache4, consts, *, layer, dec_batch, n_pages):
    grid_spec = pltpu.PrefetchScalarGridSpec(
        num_scalar_prefetch=1,
        grid=(dec_batch,),
        in_specs=[pl.BlockSpec((None, SB_HEADS, SB_W), lambda b, pt: (b, 0, 0)),
                  pl.BlockSpec((PAGE_SIZE, PAGE_SIZE), lambda b, pt: (0, 0)),
                  pl.BlockSpec((SB_HEADS, SB_W), lambda b, pt: (0, 0)),
                  pl.BlockSpec(memory_space=pl.ANY)],
        out_specs=pl.BlockSpec((None, 1, SB_W), lambda b, pt: (b, 0, 0)),
        scratch_shapes=[pltpu.VMEM((2, PAGE_SIZE, 2 * SB_W), F32), pltpu.SemaphoreType.DMA((2,))],
    )
    return pl.pallas_call(
        functools.partial(_sb_sample_kernel, layer=layer, n_pages=n_pages),
        grid_spec=grid_spec,
        out_shape=jax.ShapeDtypeStruct((dec_batch, 1, SB_W), F32),
        compiler_params=_params(("arbitrary",)),
    )(pt_flat, qbd, consts["tri"], consts["bd"], cache4)


def _rope_tables(pos):
    half = HEAD_DIM // 2
    inv = ROPE_THETA ** (-jnp.arange(half, dtype=F32) / half)
    ang = pos.astype(F32)[:, None] * inv[None, :]
    c, s = jnp.cos(ang), jnp.sin(ang)
    return jnp.tile(c, (1, 4)), jnp.tile(jnp.concatenate([-s, s], axis=1), (1, 2))


def _importance_matrix(n_blk, n_sel):
    ratio = SLC_BLOCK // CMP_STRIDE
    j = np.arange(n_blk)[:, None]
    lo = np.arange(n_sel)[None, :] * ratio
    a = ((j >= lo) & (j < lo + ratio)).astype(np.float32) + ((j + 1 >= lo) & (j + 1 < lo + ratio)).astype(np.float32)
    return a


def _constants(seq, past_len):
    n_sel = seq // SLC_BLOCK
    ncb = seq // CMP_STRIDE
    ck = min(SEL_CHUNK, seq)
    amat_p = _importance_matrix(ncb, n_sel)
    amat_p[ncb - 1:] = 0.0
    key_blk = np.arange(seq) // SLC_BLOCK
    e = (key_blk[None, :] == np.arange(n_sel)[:, None]).astype(np.float32)
    e3 = e.reshape(n_sel, seq // ck, ck).transpose(1, 0, 2)
    nc_s = past_len // CMP_STRIDE
    n_sel_s = past_len // SLC_BLOCK + 1
    ns_pad = -(-n_sel_s // LANES) * LANES
    amat_s = np.zeros((nc_s, ns_pad), np.float32)
    amat_s[:, :n_sel_s] = _importance_matrix(nc_s, n_sel_s)
    amat_s[nc_s - 1:] = 0.0
    idx = np.arange(SB_TILE)
    return {
        "amat_p": jnp.asarray(amat_p.T, BF16),
        "e3": jnp.asarray(e3, BF16),
        "eye": jnp.asarray(np.eye(LANES, dtype=np.float32), BF16),
        "amat_s": jnp.asarray(amat_s, BF16),
        "tri": jnp.asarray((idx[:, None] > idx[None, :]).astype(np.float32), BF16),
        "bd": jnp.asarray((np.arange(SB_W)[None, :] // HEAD_DIM == np.arange(SB_HEADS)[:, None]).astype(np.float32)),
    }


def _layer_weights(l, norm_mix, norm_mlp, w_in, q_norm, k_norm, cmp_pos, cmp_w1, cmp_b1, cmp_w2,
                   w_br_nsa, w_br_sb, w_gate, b_gate, w_out, w_up, w_down):
    w = w_in[l]
    c = np.cumsum([NSA_Q_W] + [NSA_KV_W] * 6 + [3 * NSA_HEADS] + [SB_W] * 3).tolist()
    col = lambda i: w[:, (0 if i == 0 else c[i - 1]):c[i]]
    wa = jnp.concatenate([col(0), col(1), col(3), col(5)], axis=1).astype(BF16)
    wb = jnp.concatenate([col(2), col(4), col(6), col(8), col(9), col(10)], axis=1).astype(BF16)
    wg = jnp.pad(col(7), ((0, 0), (0, LANES - 3 * NSA_HEADS))).astype(BF16)
    ga = jnp.concatenate([jnp.tile(q_norm[l], NSA_HEADS)] + [jnp.tile(k_norm[l, i], NSA_KV_HEADS) for i in range(3)])
    seg = np.kron(np.eye(LANES // HEAD_DIM), np.full((HEAD_DIM, HEAD_DIM), 1.0 / HEAD_DIM)).astype(np.float32)
    eye2 = jnp.eye(NSA_KV_HEADS, dtype=F32)
    w1 = cmp_w1[l].reshape(2, CMP_LEN, HEAD_DIM, CMP_HIDDEN)
    expand1 = lambda t: jnp.einsum("srdh,gk->srgdkh", t, eye2).reshape(2, CMP_STRIDE, LANES, 256).astype(BF16)
    pos = cmp_pos[l]
    expand_pos = lambda t: jnp.broadcast_to(t[:, :, None, None, :],
                                            (2, CMP_STRIDE, 1, NSA_KV_HEADS, HEAD_DIM)).reshape(2, CMP_STRIDE, 1, LANES)
    return {
        "gmix": norm_mix[l][None, :], "gmlp": norm_mlp[l][None, :],
        "wa": wa, "wb": wb, "wg": wg, "ga": ga[None, :], "seg": jnp.asarray(seg, BF16),
        "w1a": expand1(w1[:, :CMP_STRIDE]), "w1b": expand1(w1[:, CMP_STRIDE:]),
        "posa": expand_pos(pos[:, :CMP_STRIDE]), "posb": expand_pos(pos[:, CMP_STRIDE:]),
        "b1": jnp.broadcast_to(cmp_b1[l][:, None, :], (2, NSA_KV_HEADS, CMP_HIDDEN)).reshape(2, 1, 256),
        "w2": jnp.einsum("shd,gk->sghkd", cmp_w2[l], eye2).reshape(2, 256, LANES).astype(BF16),
        "wgate": w_gate[l].astype(BF16), "bgate": b_gate[l][None, :],
        "wbn": w_br_nsa[l].astype(BF16), "wbs": w_br_sb[l].astype(BF16), "wout": w_out[l].astype(BF16),
        "wup": w_up[l].astype(BF16), "wdn": w_down[l].astype(BF16),
    }


def _row_tile(m, pref):
    tm = min(pref, m)
    assert m % tm == 0
    return tm


def kernel(x_prompt, x_sample, cache_nsa, cache_sb, state_win, page_table, norm_mix, norm_mlp, w_in, q_norm, k_norm,
           cmp_pos, cmp_w1, cmp_b1, cmp_w2, w_br_nsa, w_br_sb, w_gate, b_gate, w_out, w_up, w_down):
    batch, seq, _ = x_prompt.shape
    dec_batch, dec_seq, _ = x_sample.shape
    depth = w_in.shape[0]
    n_pages = page_table.shape[1]
    past_len = n_pages * PAGE_SIZE
    win_len = state_win.shape[2]
    n_phys = cache_nsa.shape[1]
    assert dec_seq == 1 and win_len <= WINDOW and win_len <= past_len
    assert seq % SEL_CHUNK == 0 or seq < SEL_CHUNK
    assert seq % SB_TILE == 0 and seq >= WIN_KEYS and n_pages % min(CMP_PAGES, n_pages) == 0

    consts = _constants(seq, past_len)
    cos_p, sin_p = _rope_tables(jnp.arange(seq, dtype=I32))
    cos_s, sin_s = _rope_tables(jnp.full((dec_batch,), past_len, I32))
    cache_nsa4 = cache_nsa.reshape(depth, n_phys, PAGE_SIZE, 4 * NSA_KV_W)
    cache_sb4 = cache_sb.reshape(depth, n_phys, PAGE_SIZE, 2 * SB_W)
    win4 = state_win.reshape(depth, dec_batch, win_len, 2 * NSA_KV_W)
    pt_flat = page_table.reshape(-1).astype(I32)
    n_sel_s = past_len // SLC_BLOCK + 1
    n_top_s = min(SLC_TOP_N, n_sel_s)
    ns_pad = consts["amat_s"].shape[1]
    tm_p = _row_tile(batch * seq, 512)
    tm_pos = _row_tile(seq, tm_p)
    assert tm_pos == tm_p

    xp = x_prompt.reshape(batch * seq, D_MODEL)
    xs = x_sample.reshape(dec_batch, D_MODEL)
    outs = [[] for _ in range(6)]
    for l in range(depth):
        lw = _layer_weights(l, norm_mix, norm_mlp, w_in, q_norm, k_norm, cmp_pos, cmp_w1, cmp_b1, cmp_w2,
                            w_br_nsa, w_br_sb, w_gate, b_gate, w_out, w_up, w_down)
        nsa_new, sb_new, win_new, q, nsab, sbb, gates = _proj(xp, lw, cos_p, sin_p, tm=tm_p, pos_blocks=seq // tm_p)
        cb = _cmp_prompt(nsa_new, lw, batch=batch, seq=seq)
        o_nsa = _nsa_prompt(q, gates, cb, nsab, consts, batch=batch, seq=seq)
        o_sb = _sb_prompt(sbb, consts, batch=batch, seq=seq)
        xp = _merge(xp, o_nsa, o_sb, lw, tm=tm_p)
        xp = _mlp(xp, lw, tm=tm_p)
        keep = min(WINDOW, seq)
        outs[0].append(nsa_new.reshape(batch, seq, 4, NSA_KV_HEADS, HEAD_DIM))
        outs[1].append(sb_new.reshape(batch, seq, 2, SB_HEADS, HEAD_DIM))
        outs[2].append(win_new.reshape(batch, seq, 2, NSA_KV_HEADS, HEAD_DIM)[:, seq - keep:])

        nsa_s, sb_s, win_s, q_s, _, sbb_s, gates_s = _proj(xs, lw, cos_s, sin_s, tm=dec_batch, pos_blocks=1)
        cb_s = _cmp_sample(cache_nsa4, pt_flat, lw, layer=l, dec_batch=dec_batch, n_pages=n_pages)
        head_group = jnp.arange(NSA_HEADS) // NSA_GROUP
        on_group = head_group[:, None] == jnp.arange(NSA_KV_HEADS)[None, :]
        qz = jnp.where(on_group[None, :, :, None], q_s.reshape(dec_batch, NSA_HEADS, 1, HEAD_DIM),
                       jnp.zeros((), BF16)).reshape(dec_batch, NSA_HEADS, LANES)
        o_cmp, imp_s = _nsa_sample_cmp(qz, cb_s, consts, dec_batch=dec_batch, q_pos=past_len)
        imp_g = imp_s[:, ::NSA_GROUP, :].reshape(dec_batch * NSA_KV_HEADS, ns_pad)
        imp_t = jnp.pad(imp_g.T, ((0, 0), (0, LANES - dec_batch * NSA_KV_HEADS)))
        idx = _topk_sample(imp_t, n_sel=n_sel_s, n_top=n_top_s)
        idx_flat = idx[:, :dec_batch * NSA_KV_HEADS].T.reshape(-1)
        gates8 = jnp.pad(gates_s[:, :3 * NSA_HEADS].reshape(dec_batch, NSA_HEADS, 3), ((0, 0), (0, 0), (0, LANES - 3)))
        o8 = _nsa_sample_sel(idx_flat, pt_flat, qz, nsa_s, win_s, win4, o_cmp, gates8, cache_nsa4,
                             layer=l, dec_batch=dec_batch, n_pages=n_pages, n_top=n_top_s)
        o8 = o8.reshape(dec_batch, NSA_KV_HEADS, NSA_GROUP, NSA_KV_HEADS, HEAD_DIM)
        o_nsa_s = jnp.stack([o8[:, g, :, g, :] for g in range(NSA_KV_HEADS)], axis=1).reshape(dec_batch, NSA_Q_W)
        sq_s = sbb_s[:, :SB_W].reshape(dec_batch, SB_HEADS, 1, HEAD_DIM)
        eye8 = jnp.eye(SB_HEADS, dtype=bool)
        qbd = jnp.where(eye8[None, :, :, None], sq_s, jnp.zeros((), BF16)).reshape(dec_batch, SB_HEADS, SB_W)
        o_sb_s = _sb_sample(pt_flat, qbd, cache_sb4, consts, layer=l, dec_batch=dec_batch, n_pages=n_pages)
        xs = _merge(xs, o_nsa_s.astype(BF16), o_sb_s.reshape(dec_batch, SB_W).astype(BF16), lw, tm=dec_batch)
        xs = _mlp(xs, lw, tm=dec_batch)
        keep_s = min(WINDOW, win_len + 1)
        win_all = jnp.concatenate([win4[l], win_s[:, None, :]], axis=1)
        outs[3].append(nsa_s.reshape(dec_batch, 1, 4, NSA_KV_HEADS, HEAD_DIM))
        outs[4].append(sb_s.reshape(dec_batch, 1, 2, SB_HEADS, HEAD_DIM))
        outs[5].append(win_all[:, win_len + 1 - keep_s:].reshape(dec_batch, keep_s, 2, NSA_KV_HEADS, HEAD_DIM))

    return (xp.reshape(batch, seq, D_MODEL), xs.reshape(dec_batch, 1, D_MODEL)) + tuple(jnp.stack(o) for o in outs)
```

```python
import functools

import numpy as np
import jax
import jax.numpy as jnp
from jax import lax
from jax.experimental import pallas as pl
from jax.experimental.pallas import tpu as pltpu

F32 = jnp.float32
BF16 = jnp.bfloat16
I32 = jnp.int32

D_MODEL = 1024
HEAD_DIM = 64
NSA_HEADS = 8
NSA_KV_HEADS = 2
NSA_GROUP = NSA_HEADS // NSA_KV_HEADS
CMP_STRIDE = 16
CMP_LEN = 32
CMP_HIDDEN = 128
SLC_BLOCK = 64
SLC_TOP_N = 16
WINDOW = 512
SB_HEADS = 8
D_FF = 4 * D_MODEL
ROPE_THETA = 10000.0
RMS_EPS = 1e-6
FORCE_BONUS = 1e4
PAGE_SIZE = 128
SCALE = HEAD_DIM ** -0.5

LANES = 128
SUBLANES = 8
VMEM_LIMIT_BYTES = 56 * 1024 * 1024

NEG = -1e30
EXP_ZERO_BELOW = -88.0

NSA_Q_W = NSA_HEADS * HEAD_DIM
NSA_KV_W = NSA_KV_HEADS * HEAD_DIM
SB_W = SB_HEADS * HEAD_DIM
ROPED_W = NSA_Q_W + 3 * NSA_KV_W
PLAIN_W = 3 * NSA_KV_W + 3 * SB_W
Q_TILE = SLC_BLOCK
SB_TILE = 128
SEL_CHUNK = 512
WIN_KEYS = WINDOW + 2 * Q_TILE
CMP_PAGES = 16


def _dot(a, b):
    return jnp.dot(a, b, preferred_element_type=F32)


def _dot_nt(a, b):
    return lax.dot_general(a, b, (((1,), (1,)), ((), ())), preferred_element_type=F32)


def _split2(x):
    hi = x.astype(BF16)
    lo = (x - hi.astype(F32)).astype(BF16)
    return hi, lo


def _split3(x):
    hi = x.astype(BF16)
    r = x - hi.astype(F32)
    mid = r.astype(BF16)
    lo = (r - mid.astype(F32)).astype(BF16)
    return hi, mid, lo


def _rmsnorm(x, gain):
    ms = jnp.mean(x * x, axis=-1, keepdims=True)
    return x * lax.rsqrt(ms + RMS_EPS) * gain


def _softplus(z):
    return jnp.maximum(z, 0.0) + jnp.log(1.0 + jnp.exp(-jnp.abs(z)))


def _masked_softmax_parts(s, mask):
    sm = jnp.where(mask, s, NEG)
    m = jnp.max(sm, axis=-1, keepdims=True)
    p = jnp.where(mask, jnp.exp(sm - m), 0.0)
    den = jnp.maximum(jnp.sum(p, axis=-1, keepdims=True), 1e-30)
    return p, den


def _softmax_biased(s, bias):
    sm = s + bias
    m = jnp.maximum(jnp.max(sm, axis=-1, keepdims=True), 0.5 * NEG)
    p = jnp.exp(sm - m)
    return p, jnp.maximum(_row_sums(p), 1e-30)


def _fold_lanes(x, op):
    out = x[:, 0:LANES]
    for j in range(1, x.shape[1] // LANES):
        out = op(out, x[:, j * LANES:(j + 1) * LANES])
    return out


def _row_sums(x):
    ones = jnp.ones((LANES, LANES), BF16)
    return sum(_dot(part, ones) for part in _split3(_fold_lanes(x, jnp.add)))


def _params(semantics):
    return pltpu.CompilerParams(dimension_semantics=semantics, vmem_limit_bytes=VMEM_LIMIT_BYTES)


def _const_spec(shape):
    n = len(shape)
    return pl.BlockSpec(shape, lambda *_: (0,) * n)


def _proj_kernel(x_ref, gmix_ref, wa_ref, wb_ref, wg_ref, ga_ref, cos_ref, sin_ref, seg_ref,
                 nsa_ref, sbn_ref, win_ref, q_ref, nsab_ref, sbb_ref, gate_ref):
    xn = _rmsnorm(x_ref[...], gmix_ref[...]).astype(BF16)
    ha = _dot(xn, wa_ref[...])
    hb = _dot(xn, wb_ref[...])
    gate_ref[...] = jax.nn.sigmoid(_dot(xn, wg_ref[...]))
    cos = cos_ref[...]
    sin = sin_ref[...]
    seg = seg_ref[...]
    lane = lax.broadcasted_iota(I32, (1, LANES), 1)
    first_half = (lane & (HEAD_DIM - 1)) < (HEAD_DIM // 2)
    roped = []
    for j in range(ROPED_W // LANES):
        h = ha[:, j * LANES:(j + 1) * LANES]
        hi, lo = _split2(h * h)
        ms = _dot(hi, seg) + _dot(lo, seg)
        y = h * lax.rsqrt(ms + RMS_EPS) * ga_ref[:, j * LANES:(j + 1) * LANES]
        rot = jnp.where(first_half, pltpu.roll(y, LANES - HEAD_DIM // 2, 1), pltpu.roll(y, HEAD_DIM // 2, 1))
        roped.append(y * cos + rot * sin)
    nq = NSA_Q_W // LANES
    for j in range(nq):
        q_ref[:, j * LANES:(j + 1) * LANES] = (roped[j] * SCALE).astype(BF16)
    kc, ks, kw = roped[nq], roped[nq + 1], roped[nq + 2]
    vc, vs, vw = hb[:, 0:128], hb[:, 128:256], hb[:, 256:384]
    sq, sk, sv = hb[:, 384:896], hb[:, 896:1408], hb[:, 1408:1920]
    nsa_ref[:, 0:128] = kc
    nsa_ref[:, 128:256] = vc
    nsa_ref[:, 256:384] = ks
    nsa_ref[:, 384:512] = vs
    sbn_ref[:, 0:512] = sk
    sbn_ref[:, 512:1024] = sv
    win_ref[:, 0:128] = kw
    win_ref[:, 128:256] = vw
    nsab_ref[:, 0:128] = ks.astype(BF16)
    nsab_ref[:, 128:256] = vs.astype(BF16)
    nsab_ref[:, 256:384] = kw.astype(BF16)
    nsab_ref[:, 384:512] = vw.astype(BF16)
    sbb_ref[:, 0:512] = (sq * SCALE).astype(BF16)
    sbb_ref[:, 512:1024] = sk.astype(BF16)
    sbb_ref[:, 1024:1536] = sv.astype(BF16)


def _proj(x2, lw, cos, sin, *, tm, pos_blocks):
    m = x2.shape[0]
    row = lambda w: pl.BlockSpec((tm, w), lambda i: (i, 0))
    pos = pl.BlockSpec((tm, LANES), lambda i: (i % pos_blocks, 0))
    in_specs = [row(D_MODEL), _const_spec((1, D_MODEL)), _const_spec((D_MODEL, ROPED_W)),
                _const_spec((D_MODEL, PLAIN_W)), _const_spec((D_MODEL, LANES)), _const_spec((1, ROPED_W)),
                pos, pos, _const_spec((LANES, LANES))]
    widths = [(512, F32), (1024, F32), (256, F32), (512, BF16), (512, BF16), (1536, BF16), (128, F32)]
    return pl.pallas_call(
        _proj_kernel,
        grid=(m // tm,),
        in_specs=in_specs,
        out_specs=[row(w) for w, _ in widths],
        out_shape=[jax.ShapeDtypeStruct((m, w), dt) for w, dt in widths],
        compiler_params=_params(("parallel",)),
    )(x2, lw["gmix"], lw["wa"], lw["wb"], lw["wg"], lw["ga"], cos, sin, lw["seg"])


def _cmp_mlp(load_rows, slot, posa_ref, posb_ref, w1a_ref, w1b_ref, b1_ref, w2_ref):
    acc_a = acc_b = None
    for r in range(CMP_STRIDE):
        xr = load_rows(r)
        pa = _dot((xr + posa_ref[slot, r]).astype(BF16), w1a_ref[slot, r])
        pb = _dot((xr + posb_ref[slot, r]).astype(BF16), w1b_ref[slot, r])
        acc_a = pa if acc_a is None else acc_a + pa
        acc_b = pb if acc_b is None else acc_b + pb
    n = acc_a.shape[0]
    h = acc_a + pltpu.roll(acc_b, n - 1, 0) + b1_ref[slot]
    h = h * jax.nn.sigmoid(h)
    return _dot(h.astype(BF16), w2_ref[slot])


def _cmp_prompt_kernel(xk_ref, xv_ref, posa_ref, posb_ref, w1a_ref, w1b_ref, b1_ref, w2_ref, o_ref):
    nc = o_ref.shape[0]
    for slot, x_ref in enumerate((xk_ref, xv_ref)):
        load = lambda r, x_ref=x_ref: x_ref[pl.ds(r, nc, stride=CMP_STRIDE), :]
        out = _cmp_mlp(load, slot, posa_ref, posb_ref, w1a_ref, w1b_ref, b1_ref, w2_ref)
        o_ref[:, slot * LANES:(slot + 1) * LANES] = out.astype(BF16)


def _cmp_weight_specs():
    return [_const_spec((2, CMP_STRIDE, 1, LANES)), _const_spec((2, CMP_STRIDE, 1, LANES)),
            _const_spec((2, CMP_STRIDE, LANES, 256)), _const_spec((2, CMP_STRIDE, LANES, 256)),
            _const_spec((2, 1, 256)), _const_spec((2, 256, LANES))]


def _cmp_weights(lw):
    return (lw["posa"], lw["posb"], lw["w1a"], lw["w1b"], lw["b1"], lw["w2"])


def _cmp_prompt(nsa_new, lw, *, batch, seq):
    nc = seq // CMP_STRIDE
    return pl.pallas_call(
        _cmp_prompt_kernel,
        grid=(batch,),
        in_specs=[pl.BlockSpec((seq, LANES), lambda b: (b, 0)), pl.BlockSpec((seq, LANES), lambda b: (b, 1))]
        + _cmp_weight_specs(),
        out_specs=pl.BlockSpec((None, nc, 256), lambda b: (b, 0, 0)),
        out_shape=jax.ShapeDtypeStruct((batch, nc, 256), BF16),
        compiler_params=_params(("parallel",)),
    )(nsa_new, nsa_new, *_cmp_weights(lw))


def _cmp_sample_kernel(pt_ref, cache_ref, posa_ref, posb_ref, w1a_ref, w1b_ref, b1_ref, w2_ref, o_ref,
                       buf, rows_ref, sem,
                       *, layer, pages, n_pages, steps, total):
    i = pl.program_id(0)
    per_page = PAGE_SIZE // CMP_STRIDE

    def copies(j, half):
        b = j // steps
        s = j % steps
        out = []
        for k in range(pages + 1):
            p = jnp.minimum(s * pages + k, n_pages - 1)
            src = cache_ref.at[layer, pt_ref[b * n_pages + p], pl.ds(0, 2 * LANES), :]
            out.append(pltpu.make_async_copy(src, buf.at[half, k], sem.at[half]))
        return out

    @pl.when(i == 0)
    def _():
        for c in copies(i, 0):
            c.start()

    @pl.when(i + 1 < total)
    def _():
        for c in copies(i + 1, (i + 1) & 1):
            c.start()

    half = i & 1
    for c in copies(i, half):
        c.wait()
    for slot in range(2):
        for k in range(pages + 1):
            rows_ref[k] = buf[half, k, slot * LANES:(slot + 1) * LANES, :].T

        def load(r):
            rows = [rows_ref[k, pl.ds(r, per_page, stride=CMP_STRIDE), :] for k in range(pages + 1)]
            return jnp.concatenate(rows, axis=0)

        out = _cmp_mlp(load, slot, posa_ref, posb_ref, w1a_ref, w1b_ref, b1_ref, w2_ref)
        o_ref[:, slot * LANES:(slot + 1) * LANES] = out[:pages * per_page].astype(BF16)


def _cmp_sample(cache4, pt_flat, lw, *, layer, dec_batch, n_pages):
    pages = min(CMP_PAGES, n_pages)
    steps = n_pages // pages
    per_page = PAGE_SIZE // CMP_STRIDE
    nc = n_pages * per_page
    total = dec_batch * steps
    wspecs = [pl.BlockSpec(sp.block_shape, lambda i, pt, n=len(sp.block_shape): (0,) * n)
              for sp in _cmp_weight_specs()]
    grid_spec = pltpu.PrefetchScalarGridSpec(
        num_scalar_prefetch=1,
        grid=(total,),
        in_specs=[pl.BlockSpec(memory_space=pl.ANY)] + wspecs,
        out_specs=pl.BlockSpec((None, pages * per_page, 256), lambda i, pt: (i // steps, i % steps, 0)),
        scratch_shapes=[pltpu.VMEM((2, pages + 1, 2 * LANES, PAGE_SIZE), F32),
                        pltpu.VMEM((pages + 1, PAGE_SIZE, LANES), F32),
                        pltpu.SemaphoreType.DMA((2,))],
    )
    kern = functools.partial(_cmp_sample_kernel, layer=layer, pages=pages, n_pages=n_pages, steps=steps, total=total)
    return pl.pallas_call(
        kern,
        grid_spec=grid_spec,
        out_shape=jax.ShapeDtypeStruct((dec_batch, nc, 256), BF16),
        compiler_params=_params(("arbitrary",)),
    )(pt_flat, cache4, *_cmp_weights(lw))


def _beats(bi, key, srow, i):
    tie = jnp.where(srow > i, 1.0, 0.0)
    return jnp.where(bi > key, 1.0, jnp.where(bi == key, tie, 0.0))


def _rank_static(key, srow):
    parts = [jnp.zeros(key.shape, F32) for _ in range(4)]
    for i in range(key.shape[0]):
        parts[i % 4] = parts[i % 4] + _beats(key[i:i + 1, :], key, srow, i)
    return (parts[0] + parts[1]) + (parts[2] + parts[3])


def _forced(srow, cur):
    return (srow == 0) | (srow == cur) | (srow == cur - 1)


def _nsa_prompt_kernel(q_ref, gate_ref, cb_ref, kv_ref, amat_ref, e3_ref, eye_ref, gexp_ref, o_ref, s_ref,
                       *, seq, n_sel, n_top):
    t = pl.program_id(1)
    t0 = t * Q_TILE
    ncb = cb_ref.shape[0]
    rows = NSA_HEADS * Q_TILE
    lane = lax.broadcasted_iota(I32, (1, LANES), 1)
    lane_group = lane >> 6
    qpos = t0 + lax.broadcasted_iota(I32, (Q_TILE, 1), 0)
    q = q_ref[...].astype(F32)
    gates = sum(_dot(part, gexp_ref[...]) for part in _split2(gate_ref[...]))

    def q_slab(g):
        parts = []
        for r in range(NSA_GROUP):
            h = NSA_GROUP * g + r
            blk = q[:, (h // 2) * LANES:(h // 2 + 1) * LANES]
            if h % 2 != g:
                blk = pltpu.roll(blk, HEAD_DIM, 1)
            parts.append(jnp.where(lane_group == g, blk, 0.0))
        return jnp.concatenate(parts, axis=0).astype(BF16)

    q_all = jnp.concatenate([q_slab(g) for g in range(NSA_KV_HEADS)], axis=0)
    per_head = lambda x: jnp.concatenate([x] * NSA_HEADS, axis=0)

    kcb = cb_ref[:, 0:LANES]
    vcb = cb_ref[:, LANES:2 * LANES]
    cb_end = lax.broadcasted_iota(I32, (1, ncb), 1) * CMP_STRIDE + (CMP_LEN - 1)
    p, den = _softmax_biased(_dot_nt(q_all, kcb), per_head(jnp.where(cb_end <= qpos, 0.0, NEG)))
    inv = 1.0 / den
    p = p * jnp.concatenate([inv] * (ncb // LANES), axis=1)
    o_cmp = _dot(p.astype(BF16), vcb)
    imp = jnp.concatenate([sum(p[(NSA_GROUP * g + r) * Q_TILE:(NSA_GROUP * g + r + 1) * Q_TILE]
                               for r in range(NSA_GROUP)) for g in range(NSA_KV_HEADS)], axis=0)
    amat = amat_ref[...]
    imp_s = sum(_dot_nt(amat, part) for part in _split3(imp))
    srow = lax.broadcasted_iota(I32, (n_sel, 1), 0)
    qpos_lane = t0 + (lane & (Q_TILE - 1))
    valid = srow * SLC_BLOCK <= qpos_lane
    bonus = jnp.where(_forced(srow, t), FORCE_BONUS, 0.0)
    key = jnp.where(valid, imp_s + bonus, -jnp.inf)
    sel_t = jnp.where(_rank_static(key, srow) < n_top, 1.0, 0.0).astype(BF16)
    sel_q = _dot_nt(eye_ref[...], sel_t).astype(BF16)

    ck = e3_ref.shape[2]
    kcol = lax.broadcasted_iota(I32, (1, ck), 1)
    n_chunks = t // (ck // Q_TILE) + 1

    def sel_branch():
        qpos2 = jnp.concatenate([qpos] * NSA_KV_HEADS, axis=0)

        def scores(c, mrun):
            k0 = pl.multiple_of(c * ck, ck)
            allow = (_dot(sel_q, e3_ref[c]) > 0.5) & ((k0 + kcol) <= qpos2)
            bias = jnp.where(allow, 0.0, NEG)
            bias = jnp.concatenate([bias[:Q_TILE]] * NSA_GROUP + [bias[Q_TILE:]] * NSA_GROUP, axis=0)
            s = _dot_nt(q_all, kv_ref[pl.ds(k0, ck), 0:LANES]) + bias
            s_ref[c] = s
            return jnp.maximum(mrun, _fold_lanes(s, jnp.maximum))

        mrun = lax.fori_loop(0, n_chunks, scores, jnp.full((rows, LANES), NEG, F32))
        m = jnp.max(mrun, axis=-1, keepdims=True)

        def weigh(c, carry):
            lrun, acc = carry
            k0 = pl.multiple_of(c * ck, ck)
            p = jnp.exp(s_ref[c] - m)
            acc = acc + _dot(p.astype(BF16), kv_ref[pl.ds(k0, ck), LANES:2 * LANES])
            return lrun + _fold_lanes(p, jnp.add), acc

        zeros = jnp.zeros((rows, LANES), F32)
        lrun, acc = lax.fori_loop(0, n_chunks, weigh, (zeros, zeros))
        return acc / _row_sums(lrun)

    wk = min(WIN_KEYS, seq)
    start = pl.multiple_of(jnp.clip(t0 - WINDOW, 0, seq - wk), Q_TILE)
    kwin = kv_ref[pl.ds(start, wk), 2 * LANES:3 * LANES]
    vwin = kv_ref[pl.ds(start, wk), 3 * LANES:4 * LANES]
    diff = qpos - (start + lax.broadcasted_iota(I32, (1, wk), 1))
    wbias = jnp.where((diff >= 0) & (diff <= WINDOW), 0.0, NEG)
    p, den = _softmax_biased(_dot_nt(q_all, kwin), per_head(wbias))
    o_win = _dot(p.astype(BF16), vwin) / den

    o_sel = sel_branch()
    for g in range(NSA_KV_HEADS):
        heads = []
        for r in range(NSA_GROUP):
            h = NSA_GROUP * g + r
            sl = slice(h * Q_TILE, (h + 1) * Q_TILE)
            gc = [gates[:, (3 * h + c) * LANES:(3 * h + c + 1) * LANES] for c in range(3)]
            oh = gc[0] * o_cmp[sl] + gc[1] * o_sel[sl] + gc[2] * o_win[sl]
            if h % 2 != g:
                oh = pltpu.roll(oh, HEAD_DIM, 1)
            heads.append(oh)
        for pr in range(NSA_GROUP // 2):
            hp = (NSA_GROUP * g) // 2 + pr
            pair = jnp.where(lane_group == 0, heads[2 * pr], heads[2 * pr + 1])
            o_ref[:, hp * LANES:(hp + 1) * LANES] = pair.astype(BF16)


def _nsa_prompt(q, gates, cb, nsab, consts, *, batch, seq):
    nt = seq // Q_TILE
    ncb = seq // CMP_STRIDE
    n_sel = seq // SLC_BLOCK
    ck = min(SEL_CHUNK, seq)
    kern = functools.partial(_nsa_prompt_kernel, seq=seq, n_sel=n_sel, n_top=min(SLC_TOP_N, n_sel))
    tile = lambda w: pl.BlockSpec((Q_TILE, w), lambda b, t: (b * nt + t, 0))
    return pl.pallas_call(
        kern,
        grid=(batch, nt),
        in_specs=[tile(NSA_Q_W), tile(LANES),
                  pl.BlockSpec((None, ncb, 256), lambda b, t: (b, 0, 0)),
                  pl.BlockSpec((seq, 512), lambda b, t: (b, 0)),
                  _const_spec((n_sel, ncb)), _const_spec((seq // ck, n_sel, ck)), _const_spec((LANES, LANES)),
                  _const_spec((LANES, 3 * NSA_HEADS * LANES))],
        out_specs=tile(NSA_Q_W),
        out_shape=jax.ShapeDtypeStruct((batch * seq, NSA_Q_W), BF16),
        scratch_shapes=[pltpu.VMEM((seq // ck, NSA_HEADS * Q_TILE, ck), F32)],
        compiler_params=_params(("parallel", "arbitrary")),
    )(q, gates, cb, nsab, consts["amat_p"], consts["e3"], consts["eye"], consts["gexp"])


def _sb_sums(sps, tri_ones, mask):
    n = sps[0].shape[0]
    parts = []
    for sp in sps:
        lk = -sp if mask is None else jnp.where(mask, -sp, 0.0)
        parts.extend(_split2(lk))
    if n % 16 == 0:
        sums = _dot(jnp.concatenate(parts, axis=0), tri_ones)
        terms = [sums[i * n:(i + 1) * n] for i in range(len(parts))]
    else:
        terms = [_dot(part, tri_ones) for part in parts]
    out = []
    for i in range(len(sps)):
        both = terms[2 * i] + terms[2 * i + 1]
        out.append((both[:, :LANES], both[:, LANES:]))
    return out


def _sb_block(qh, k, v, tri_ones, carry, acc, kv_transposed=False):
    z = _dot(qh, k) if kv_transposed else _dot_nt(qh, k)
    sp = _softplus(z)
    (later, total), = _sb_sums([sp], tri_ones, None)
    a = jnp.exp(z - sp + later + carry).astype(BF16)
    return carry + total, acc + (_dot_nt(a, v) if kv_transposed else _dot(a, v))


def _sb_prompt_kernel(q_ref, k_ref, v_ref, tri_ref, o_ref, qh_ref, carry_ref, acc_ref):
    qi = pl.program_id(1)
    tq = SB_TILE
    tri = tri_ref[...]
    lane_half = lax.broadcasted_iota(I32, (1, LANES), 1) >> 6
    row = lax.broadcasted_iota(I32, (tq, 1), 0)
    col = lax.broadcasted_iota(I32, (1, tq), 1)
    diag_mask = col < row
    for h in range(SB_HEADS):
        qp = q_ref[:, (h // 2) * LANES:(h // 2 + 1) * LANES]
        qh_ref[h] = jnp.where(lane_half == h % 2, qp, jnp.zeros_like(qp))
        carry_ref[h] = jnp.zeros((tq, LANES), F32)
        acc_ref[h] = jnp.zeros((tq, LANES), F32)

    def visit(kb, mask):
        k0 = pl.multiple_of(kb * tq, tq)
        zs = []
        for h in range(SB_HEADS):
            k = k_ref[pl.ds(k0, tq), (h // 2) * LANES:(h // 2 + 1) * LANES]
            zs.append(_dot_nt(qh_ref[h], k))
        sps = [_softplus(z) for z in zs]
        sums = _sb_sums(sps, tri, mask)
        worst = None
        for pair in range(SB_HEADS // 2):
            ws = []
            for h in (2 * pair, 2 * pair + 1):
                a = jnp.exp(zs[h] - sps[h] + sums[h][0] + carry_ref[h])
                ws.append((a if mask is None else jnp.where(mask, a, 0.0)).astype(BF16))
                carry = carry_ref[h] + sums[h][1]
                carry_ref[h] = carry
                worst = carry if worst is None else jnp.maximum(worst, carry)
            v = v_ref[pl.ds(k0, tq), pair * LANES:(pair + 1) * LANES]
            pv = _dot(jnp.concatenate(ws, axis=0), v)
            acc_ref[2 * pair] += pv[:tq]
            acc_ref[2 * pair + 1] += pv[tq:]
        return (jnp.max(worst) > EXP_ZERO_BELOW).astype(I32)

    def cond(st):
        kb, go = st
        return (kb >= 0) & (go > 0)

    def body(st):
        kb, _ = st
        return kb - 1, visit(kb, None)

    lax.while_loop(cond, body, (qi - 1, visit(qi, diag_mask)))
    for pair in range(SB_HEADS // 2):
        o_ref[:, pair * LANES:(pair + 1) * LANES] = jnp.where(
            lane_half == 0, acc_ref[2 * pair], acc_ref[2 * pair + 1]).astype(BF16)


def _sb_prompt(sbb, consts, *, batch, seq):
    nq = seq // SB_TILE
    return pl.pallas_call(
        _sb_prompt_kernel,
        grid=(batch, nq),
        in_specs=[pl.BlockSpec((SB_TILE, SB_W), lambda b, qi: (b * nq + qi, 0)),
                  pl.BlockSpec((seq, SB_W), lambda b, qi: (b, 1)),
                  pl.BlockSpec((seq, SB_W), lambda b, qi: (b, 2)),
                  _const_spec((SB_TILE, SB_TILE + LANES))],
        out_specs=pl.BlockSpec((SB_TILE, SB_W), lambda b, qi: (b * nq + qi, 0)),
        out_shape=jax.ShapeDtypeStruct((batch * seq, SB_W), BF16),
        scratch_shapes=[pltpu.VMEM((SB_HEADS, SB_TILE, LANES), BF16),
                        pltpu.VMEM((SB_HEADS, SB_TILE, LANES), F32),
                        pltpu.VMEM((SB_HEADS, SB_TILE, LANES), F32)],
        compiler_params=_params(("parallel", "arbitrary")),
    )(sbb, sbb, sbb, consts["tri"])


def _merge_kernel(x_ref, on_ref, os_ref, gmix_ref, wgate_ref, bgate_ref, wbn_ref, wbs_ref, wout_ref, o_ref):
    x = x_ref[...]
    xn = _rmsnorm(x, gmix_ref[...]).astype(BF16)
    g = jax.nn.sigmoid(_dot(xn, wgate_ref[...]) + bgate_ref[...])
    u = g[:, :D_MODEL] * _dot(on_ref[...], wbn_ref[...]) + g[:, D_MODEL:] * _dot(os_ref[...], wbs_ref[...])
    o_ref[...] = x + _dot(u.astype(BF16), wout_ref[...])


def _mlp_kernel(x_ref, gmlp_ref, wup_ref, wdn_ref, o_ref):
    x = x_ref[...]
    h = _rmsnorm(x, gmlp_ref[...]).astype(BF16)
    acc = None
    step = 1024
    for c in range(D_FF // step):
        up = _dot(h, wup_ref[:, c * step:(c + 1) * step])
        up = jnp.square(jnp.maximum(up, 0.0)).astype(BF16)
        part = _dot(up, wdn_ref[c * step:(c + 1) * step, :])
        acc = part if acc is None else acc + part
    o_ref[...] = x + acc


def _resident(shape):
    n = len(shape)
    return pl.BlockSpec(shape, lambda *_: (0,) * n, pipeline_mode=pl.Buffered(1))


def _merge(x2, o_nsa, o_sb, lw, *, tm):
    m = x2.shape[0]
    row = lambda w: pl.BlockSpec((tm, w), lambda i: (i, 0))
    return pl.pallas_call(
        _merge_kernel,
        grid=(m // tm,),
        in_specs=[row(D_MODEL), row(NSA_Q_W), row(SB_W), _resident((1, D_MODEL)),
                  _resident((D_MODEL, 2 * D_MODEL)), _resident((1, 2 * D_MODEL)),
                  _resident((NSA_Q_W, D_MODEL)), _resident((SB_W, D_MODEL)), _resident((D_MODEL, D_MODEL))],
        out_specs=row(D_MODEL),
        out_shape=jax.ShapeDtypeStruct((m, D_MODEL), F32),
        compiler_params=_params(("parallel",)),
    )(x2, o_nsa, o_sb, lw["gmix"], lw["wgate"], lw["bgate"], lw["wbn"], lw["wbs"], lw["wout"])


def _mlp(x2, lw, *, tm):
    m = x2.shape[0]
    row = pl.BlockSpec((tm, D_MODEL), lambda i: (i, 0))
    return pl.pallas_call(
        _mlp_kernel,
        grid=(m // tm,),
        in_specs=[row, _resident((1, D_MODEL)), _resident((D_MODEL, D_FF)), _resident((D_FF, D_MODEL))],
        out_specs=row,
        out_shape=jax.ShapeDtypeStruct((m, D_MODEL), F32),
        compiler_params=_params(("parallel",)),
    )(x2, lw["gmlp"], lw["wup"], lw["wdn"])


def _nsa_sample_cmp_kernel(qz_ref, cb_ref, amat_ref, oc_ref, imps_ref, *, q_pos):
    qz = qz_ref[...]
    nc = cb_ref.shape[0]
    kcb = cb_ref[:, 0:LANES]
    vcb = cb_ref[:, LANES:2 * LANES]
    cb_end = lax.broadcasted_iota(I32, (1, nc), 1) * CMP_STRIDE + (CMP_LEN - 1)
    p, den = _masked_softmax_parts(_dot_nt(qz, kcb), cb_end <= q_pos)
    p = p / den
    oc_ref[...] = _dot(p.astype(BF16), vcb)
    row = lax.broadcasted_iota(I32, (NSA_HEADS, 1), 0)
    g0 = jnp.sum(p[0:NSA_GROUP], axis=0, keepdims=True)
    g1 = jnp.sum(p[NSA_GROUP:], axis=0, keepdims=True)
    imp = jnp.where(row < NSA_GROUP, g0, g1)
    amat = amat_ref[...]
    imps_ref[...] = sum(_dot(part, amat) for part in _split3(imp))


def _nsa_sample_cmp(qz, cb, consts, *, dec_batch, q_pos):
    nc = cb.shape[1]
    ns_pad = consts["amat_s"].shape[1]
    return pl.pallas_call(
        functools.partial(_nsa_sample_cmp_kernel, q_pos=q_pos),
        grid=(dec_batch,),
        in_specs=[pl.BlockSpec((None, NSA_HEADS, LANES), lambda b: (b, 0, 0)),
                  pl.BlockSpec((None, nc, 256), lambda b: (b, 0, 0)),
                  _const_spec((nc, ns_pad))],
        out_specs=[pl.BlockSpec((None, NSA_HEADS, LANES), lambda b: (b, 0, 0)),
                   pl.BlockSpec((None, NSA_HEADS, ns_pad), lambda b: (b, 0, 0))],
        out_shape=[jax.ShapeDtypeStruct((dec_batch, NSA_HEADS, LANES), F32),
                   jax.ShapeDtypeStruct((dec_batch, NSA_HEADS, ns_pad), F32)],
        compiler_params=_params(("parallel",)),
    )(qz, cb, consts["amat_s"])


def _topk_sample_kernel(imp_ref, idx_ref, key_ref, *, n_sel, n_top):
    nrows = imp_ref.shape[0]
    srow = lax.broadcasted_iota(I32, (nrows, 1), 0)
    bonus = jnp.where(_forced(srow, n_sel - 1), FORCE_BONUS, 0.0)
    key = jnp.where(srow < n_sel, imp_ref[...] + bonus, -jnp.inf)
    key_ref[...] = key

    def body(i, cnt):
        return cnt + _beats(key_ref[pl.ds(i, 1), :], key, srow, i)

    cnt = lax.fori_loop(0, n_sel, body, jnp.zeros(key.shape, F32))
    srow_f = srow.astype(F32)
    picks = [jnp.sum(jnp.where(cnt == float(k), srow_f, 0.0), axis=0, keepdims=True) for k in range(n_top)]
    idx_ref[...] = jnp.concatenate(picks, axis=0).astype(I32)


def _topk_sample(imp_t, *, n_sel, n_top):
    nrows = imp_t.shape[0]
    return pl.pallas_call(
        functools.partial(_topk_sample_kernel, n_sel=n_sel, n_top=n_top),
        out_shape=jax.ShapeDtypeStruct((n_top, LANES), I32),
        scratch_shapes=[pltpu.VMEM((nrows, LANES), F32)],
    )(imp_t)


def _nsa_sample_sel_kernel(idx_ref, pt_ref, qz_ref, own_ref, wown_ref, win_ref, oc_ref, gate_ref, cache_ref,
                           o_ref, buf, sem, *, layer, n_pages, n_top):
    b = pl.program_id(0)
    per_page = PAGE_SIZE // SLC_BLOCK
    n_past_blk = n_pages * per_page

    def blk_of(g, k):
        return idx_ref[(b * NSA_KV_HEADS + g) * n_top + k]

    def copy(g, k):
        blk = jnp.minimum(blk_of(g, k), n_past_blk - 1)
        page = pt_ref[b * n_pages + blk // per_page]
        src = cache_ref.at[layer, page, pl.ds(2 * LANES, 2 * LANES), :]
        return pltpu.make_async_copy(src, buf.at[g, k], sem.at[g, k])

    for g in range(NSA_KV_HEADS):
        for k in range(n_top):
            copy(g, k).start()

    qz = qz_ref[...]
    qf = qz.astype(F32)
    row = lax.broadcasted_iota(I32, (NSA_HEADS, 1), 0)
    top_rows = row < NSA_GROUP
    n_keys = n_top * PAGE_SIZE
    key_lane = lax.broadcasted_iota(I32, (1, n_keys), 1)
    kslot = key_lane >> 7
    khalf = (key_lane >> 6) & (per_page - 1)

    for g in range(NSA_KV_HEADS):
        for k in range(n_top):
            copy(g, k).wait()

    s_parts, m_parts, own_parts, v_parts = [], [], [], []
    for g in range(NSA_KV_HEADS):
        kt = jnp.concatenate([buf[g, k, 0:LANES, :] for k in range(n_top)], axis=1).astype(BF16)
        v_parts.append(jnp.concatenate([buf[g, k, LANES:2 * LANES, :] for k in range(n_top)], axis=1).astype(BF16))
        s_parts.append(_dot(qz, kt))
        msk = jnp.zeros((1, n_keys), I32)
        has_own = jnp.int32(0)
        for k in range(n_top):
            blk = blk_of(g, k)
            in_past = (blk < n_past_blk).astype(I32)
            msk = jnp.where(kslot == k, jnp.where(khalf == blk % per_page, in_past, 0), msk)
            has_own = jnp.maximum(has_own, 1 - in_past)
        m_parts.append(msk)
        own_parts.append(has_own)
    s = jnp.where(top_rows, s_parts[0], s_parts[1])
    mask = jnp.where(top_rows, m_parts[0], m_parts[1]) > 0
    own_ok = jnp.where(top_rows, own_parts[0], own_parts[1]) > 0
    own = own_ref[...]
    s_own = jnp.sum(qf * own[:, 2 * LANES:3 * LANES], axis=-1, keepdims=True)
    sm = jnp.where(mask, s, NEG)
    m = jnp.maximum(jnp.max(sm, axis=-1, keepdims=True), jnp.where(own_ok, s_own, NEG))
    p = jnp.where(mask, jnp.exp(sm - m), 0.0)
    p_own = jnp.where(own_ok, jnp.exp(s_own - m), 0.0)
    den = jnp.maximum(jnp.sum(p, axis=-1, keepdims=True) + p_own, 1e-30)
    pb = p.astype(BF16)
    o_sel = jnp.where(top_rows, _dot_nt(pb, v_parts[0]), _dot_nt(pb, v_parts[1]))
    o_sel = (o_sel + p_own * own[:, 3 * LANES:4 * LANES]) / den

    wown = wown_ref[...]
    kwin = win_ref[0:LANES, :].astype(BF16)
    vwin = win_ref[LANES:2 * LANES, :].astype(BF16)
    sw = _dot(qz, kwin)
    sw_own = jnp.sum(qf * wown[:, 0:LANES], axis=-1, keepdims=True)
    mw = jnp.maximum(jnp.max(sw, axis=-1, keepdims=True), sw_own)
    pw = jnp.exp(sw - mw)
    pw_own = jnp.exp(sw_own - mw)
    denw = jnp.sum(pw, axis=-1, keepdims=True) + pw_own
    o_win = (_dot_nt(pw.astype(BF16), vwin) + pw_own * wown[:, LANES:2 * LANES]) / denw

    gates = gate_ref[...]
    o_ref[...] = gates[:, 0:1] * oc_ref[...] + gates[:, 1:2] * o_sel + gates[:, 2:3] * o_win


def _nsa_sample_sel(idx_flat, pt_flat, qz, nsa_new, win_new, win_t, o_cmp, gates8, cache_t,
                    *, layer, dec_batch, n_pages, n_top):
    win_len = win_t.shape[3]
    per_b = lambda shape: pl.BlockSpec((None,) + shape, lambda b, *_: (b,) + (0,) * len(shape))
    grid_spec = pltpu.PrefetchScalarGridSpec(
        num_scalar_prefetch=2,
        grid=(dec_batch,),
        in_specs=[per_b((NSA_HEADS, LANES)), per_b((1, 512)), per_b((1, 256)),
                  pl.BlockSpec((None, None, 2 * LANES, win_len), lambda b, *_: (layer, b, 0, 0)),
                  per_b((NSA_HEADS, LANES)), per_b((NSA_HEADS, LANES)),
                  pl.BlockSpec(memory_space=pl.ANY)],
        out_specs=per_b((NSA_HEADS, LANES)),
        scratch_shapes=[pltpu.VMEM((NSA_KV_HEADS, n_top, 2 * LANES, PAGE_SIZE), F32),
                        pltpu.SemaphoreType.DMA((NSA_KV_HEADS, n_top))],
    )
    return pl.pallas_call(
        functools.partial(_nsa_sample_sel_kernel, layer=layer, n_pages=n_pages, n_top=n_top),
        grid_spec=grid_spec,
        out_shape=jax.ShapeDtypeStruct((dec_batch, NSA_HEADS, LANES), F32),
        compiler_params=_params(("arbitrary",)),
    )(idx_flat, pt_flat, qz, nsa_new.reshape(dec_batch, 1, 512), win_new.reshape(dec_batch, 1, 256), win_t,
      o_cmp, gates8, cache_t)


def _sb_sample_kernel(pt_ref, qbd_ref, tri_ref, bd_ref, cache_ref, o_ref, buf, sem, *, layer, n_pages):
    b = pl.program_id(0)

    def copy(p, slot):
        return pltpu.make_async_copy(cache_ref.at[layer, pt_ref[b * n_pages + p]], buf.at[slot], sem.at[slot])

    copy(n_pages - 1, 0).start()
    qbd = qbd_ref[...]
    tri = tri_ref[...]

    def cond(st):
        p, _, go, _, _ = st
        return (p >= 0) & (go > 0)

    def body(st):
        p, it, _, carry, acc = st
        slot = it & 1

        @pl.when(p >= 1)
        def _():
            copy(p - 1, 1 - slot).start()

        copy(p, slot).wait()
        k = buf[slot, 0:SB_W, :].astype(BF16)
        v = buf[slot, SB_W:2 * SB_W, :].astype(BF16)
        carry, acc = _sb_block(qbd, k, v, tri, carry, acc, kv_transposed=True)
        go = (jnp.max(carry) > EXP_ZERO_BELOW).astype(I32)
        return p - 1, it + 1, go, carry, acc

    init = (jnp.int32(n_pages - 1), jnp.int32(0), jnp.int32(1),
            jnp.zeros((SB_HEADS, LANES), F32), jnp.zeros((SB_HEADS, SB_W), F32))
    p_end, it_end, _, _, acc = lax.while_loop(cond, body, init)

    @pl.when(p_end >= 0)
    def _():
        copy(p_end, it_end & 1).wait()

    o_ref[...] = jnp.sum(acc * bd_ref[...], axis=0, keepdims=True)


def _sb_sample(pt_flat, qbd, cache4, consts, *, layer, dec_batch, n_pages):
    grid_spec = pltpu.PrefetchScalarGridSpec(
        num_scalar_prefetch=1,
        grid=(dec_batch,),
        in_specs=[pl.BlockSpec((None, SB_HEADS, SB_W), lambda b, pt: (b, 0, 0)),
                  pl.BlockSpec((PAGE_SIZE, PAGE_SIZE + LANES), lambda b, pt: (0, 0)),
                  pl.BlockSpec((SB_HEADS, SB_W), lambda b, pt: (0, 0)),
                  pl.BlockSpec(memory_space=pl.ANY)],
        out_specs=pl.BlockSpec((None, 1, SB_W), lambda b, pt: (b, 0, 0)),
        scratch_shapes=[pltpu.VMEM((2, 2 * SB_W, PAGE_SIZE), F32), pltpu.SemaphoreType.DMA((2,))],
    )
    return pl.pallas_call(
        functools.partial(_sb_sample_kernel, layer=layer, n_pages=n_pages),
        grid_spec=grid_spec,
        out_shape=jax.ShapeDtypeStruct((dec_batch, 1, SB_W), F32),
        compiler_params=_params(("arbitrary",)),
    )(pt_flat, qbd, consts["tri"], consts["bd"], cache4)


def _rope_tables(pos):
    half = HEAD_DIM // 2
    inv = ROPE_THETA ** (-jnp.arange(half, dtype=F32) / half)
    ang = pos.astype(F32)[:, None] * inv[None, :]
    c, s = jnp.cos(ang), jnp.sin(ang)
    return jnp.tile(c, (1, 4)), jnp.tile(jnp.concatenate([-s, s], axis=1), (1, 2))


def _importance_matrix(n_blk, n_sel):
    ratio = SLC_BLOCK // CMP_STRIDE
    j = np.arange(n_blk)[:, None]
    lo = np.arange(n_sel)[None, :] * ratio
    a = ((j >= lo) & (j < lo + ratio)).astype(np.float32) + ((j + 1 >= lo) & (j + 1 < lo + ratio)).astype(np.float32)
    return a


def _constants(seq, past_len):
    n_sel = seq // SLC_BLOCK
    ncb = seq // CMP_STRIDE
    ck = min(SEL_CHUNK, seq)
    amat_p = _importance_matrix(ncb, n_sel)
    amat_p[ncb - 1:] = 0.0
    key_blk = np.arange(seq) // SLC_BLOCK
    e = (key_blk[None, :] == np.arange(n_sel)[:, None]).astype(np.float32)
    e3 = e.reshape(n_sel, seq // ck, ck).transpose(1, 0, 2)
    nc_s = past_len // CMP_STRIDE
    n_sel_s = past_len // SLC_BLOCK + 1
    ns_pad = -(-n_sel_s // LANES) * LANES
    amat_s = np.zeros((nc_s, ns_pad), np.float32)
    amat_s[:, :n_sel_s] = _importance_matrix(nc_s, n_sel_s)
    amat_s[nc_s - 1:] = 0.0
    idx = np.arange(SB_TILE)
    return {
        "amat_p": jnp.asarray(amat_p.T, BF16),
        "e3": jnp.asarray(e3, BF16),
        "eye": jnp.asarray(np.eye(LANES, dtype=np.float32), BF16),
        "gexp": jnp.asarray(np.kron(np.eye(LANES, 3 * NSA_HEADS, dtype=np.float32), np.ones((1, LANES), np.float32)), BF16),
        "amat_s": jnp.asarray(amat_s, BF16),
        "tri": jnp.asarray(np.concatenate([(idx[:, None] > idx[None, :]).astype(np.float32),
                                           np.ones((SB_TILE, LANES), np.float32)], axis=1), BF16),
        "bd": jnp.asarray((np.arange(SB_W)[None, :] // HEAD_DIM == np.arange(SB_HEADS)[:, None]).astype(np.float32)),
    }


def _layer_weights(l, norm_mix, norm_mlp, w_in, q_norm, k_norm, cmp_pos, cmp_w1, cmp_b1, cmp_w2,
                   w_br_nsa, w_br_sb, w_gate, b_gate, w_out, w_up, w_down):
    w = w_in[l]
    c = np.cumsum([NSA_Q_W] + [NSA_KV_W] * 6 + [3 * NSA_HEADS] + [SB_W] * 3).tolist()
    col = lambda i: w[:, (0 if i == 0 else c[i - 1]):c[i]]
    wa = jnp.concatenate([col(0), col(1), col(3), col(5)], axis=1).astype(BF16)
    wb = jnp.concatenate([col(2), col(4), col(6), col(8), col(9), col(10)], axis=1).astype(BF16)
    wg = jnp.pad(col(7), ((0, 0), (0, LANES - 3 * NSA_HEADS))).astype(BF16)
    ga = jnp.concatenate([jnp.tile(q_norm[l], NSA_HEADS)] + [jnp.tile(k_norm[l, i], NSA_KV_HEADS) for i in range(3)])
    seg = np.kron(np.eye(LANES // HEAD_DIM), np.full((HEAD_DIM, HEAD_DIM), 1.0 / HEAD_DIM)).astype(np.float32)
    eye2 = jnp.eye(NSA_KV_HEADS, dtype=F32)
    w1 = cmp_w1[l].reshape(2, CMP_LEN, HEAD_DIM, CMP_HIDDEN)
    expand1 = lambda t: jnp.einsum("srdh,gk->srgdkh", t, eye2).reshape(2, CMP_STRIDE, LANES, 256).astype(BF16)
    pos = cmp_pos[l]
    expand_pos = lambda t: jnp.broadcast_to(t[:, :, None, None, :],
                                            (2, CMP_STRIDE, 1, NSA_KV_HEADS, HEAD_DIM)).reshape(2, CMP_STRIDE, 1, LANES)
    return {
        "gmix": norm_mix[l][None, :], "gmlp": norm_mlp[l][None, :],
        "wa": wa, "wb": wb, "wg": wg, "ga": ga[None, :], "seg": jnp.asarray(seg, BF16),
        "w1a": expand1(w1[:, :CMP_STRIDE]), "w1b": expand1(w1[:, CMP_STRIDE:]),
        "posa": expand_pos(pos[:, :CMP_STRIDE]), "posb": expand_pos(pos[:, CMP_STRIDE:]),
        "b1": jnp.broadcast_to(cmp_b1[l][:, None, :], (2, NSA_KV_HEADS, CMP_HIDDEN)).reshape(2, 1, 256),
        "w2": jnp.einsum("shd,gk->sghkd", cmp_w2[l], eye2).reshape(2, 256, LANES).astype(BF16),
        "wgate": w_gate[l].astype(BF16), "bgate": b_gate[l][None, :],
        "wbn": w_br_nsa[l].astype(BF16), "wbs": w_br_sb[l].astype(BF16), "wout": w_out[l].astype(BF16),
        "wup": w_up[l].astype(BF16), "wdn": w_down[l].astype(BF16),
    }


def _row_tile(m, pref):
    tm = min(pref, m)
    assert m % tm == 0
    return tm


def kernel(x_prompt, x_sample, cache_nsa, cache_sb, state_win, page_table, norm_mix, norm_mlp, w_in, q_norm, k_norm,
           cmp_pos, cmp_w1, cmp_b1, cmp_w2, w_br_nsa, w_br_sb, w_gate, b_gate, w_out, w_up, w_down):
    batch, seq, _ = x_prompt.shape
    dec_batch, dec_seq, _ = x_sample.shape
    depth = w_in.shape[0]
    n_pages = page_table.shape[1]
    past_len = n_pages * PAGE_SIZE
    win_len = state_win.shape[2]
    n_phys = cache_nsa.shape[1]
    assert dec_seq == 1 and win_len <= WINDOW and win_len <= past_len
    assert seq % SEL_CHUNK == 0 and seq % (CMP_STRIDE * LANES) == 0
    assert seq % SB_TILE == 0 and seq >= WIN_KEYS and n_pages % min(CMP_PAGES, n_pages) == 0

    consts = _constants(seq, past_len)
    cos_p, sin_p = _rope_tables(jnp.arange(seq, dtype=I32))
    cos_s, sin_s = _rope_tables(jnp.full((dec_batch,), past_len, I32))
    rows_last = (0, 1, 3, 4, 5, 2)
    cache_nsa4 = jnp.transpose(cache_nsa, rows_last).reshape(depth, n_phys, 4 * NSA_KV_W, PAGE_SIZE)
    cache_sb4 = jnp.transpose(cache_sb, rows_last).reshape(depth, n_phys, 2 * SB_W, PAGE_SIZE)
    win4 = jnp.transpose(state_win, rows_last).reshape(depth, dec_batch, 2 * NSA_KV_W, win_len)
    pt_flat = page_table.reshape(-1).astype(I32)
    n_sel_s = past_len // SLC_BLOCK + 1
    n_top_s = min(SLC_TOP_N, n_sel_s)
    ns_pad = consts["amat_s"].shape[1]
    tm_p = _row_tile(batch * seq, 512)
    tm_pos = _row_tile(seq, tm_p)
    assert tm_pos == tm_p

    xp = x_prompt.reshape(batch * seq, D_MODEL)
    xs = x_sample.reshape(dec_batch, D_MODEL)
    outs = [[] for _ in range(6)]
    for l in range(depth):
        lw = _layer_weights(l, norm_mix, norm_mlp, w_in, q_norm, k_norm, cmp_pos, cmp_w1, cmp_b1, cmp_w2,
                            w_br_nsa, w_br_sb, w_gate, b_gate, w_out, w_up, w_down)
        nsa_new, sb_new, win_new, q, nsab, sbb, gates = _proj(xp, lw, cos_p, sin_p, tm=tm_p, pos_blocks=seq // tm_p)
        cb = _cmp_prompt(nsa_new, lw, batch=batch, seq=seq)
        o_nsa = _nsa_prompt(q, gates, cb, nsab, consts, batch=batch, seq=seq)
        o_sb = _sb_prompt(sbb, consts, batch=batch, seq=seq)
        xp = _merge(xp, o_nsa, o_sb, lw, tm=tm_p)
        xp = _mlp(xp, lw, tm=tm_p)
        keep = min(WINDOW, seq)
        outs[0].append(nsa_new.reshape(batch, seq, 4, NSA_KV_HEADS, HEAD_DIM))
        outs[1].append(sb_new.reshape(batch, seq, 2, SB_HEADS, HEAD_DIM))
        outs[2].append(win_new.reshape(batch, seq, 2, NSA_KV_HEADS, HEAD_DIM)[:, seq - keep:])

        nsa_s, sb_s, win_s, q_s, _, sbb_s, gates_s = _proj(xs, lw, cos_s, sin_s, tm=dec_batch, pos_blocks=1)
        cb_s = _cmp_sample(cache_nsa4, pt_flat, lw, layer=l, dec_batch=dec_batch, n_pages=n_pages)
        head_group = jnp.arange(NSA_HEADS) // NSA_GROUP
        on_group = head_group[:, None] == jnp.arange(NSA_KV_HEADS)[None, :]
        qz = jnp.where(on_group[None, :, :, None], q_s.reshape(dec_batch, NSA_HEADS, 1, HEAD_DIM),
                       jnp.zeros((), BF16)).reshape(dec_batch, NSA_HEADS, LANES)
        o_cmp, imp_s = _nsa_sample_cmp(qz, cb_s, consts, dec_batch=dec_batch, q_pos=past_len)
        imp_g = imp_s[:, ::NSA_GROUP, :].reshape(dec_batch * NSA_KV_HEADS, ns_pad)
        imp_t = jnp.pad(imp_g.T, ((0, 0), (0, LANES - dec_batch * NSA_KV_HEADS)))
        idx = _topk_sample(imp_t, n_sel=n_sel_s, n_top=n_top_s)
        idx_flat = idx[:, :dec_batch * NSA_KV_HEADS].T.reshape(-1)
        gates8 = jnp.pad(gates_s[:, :3 * NSA_HEADS].reshape(dec_batch, NSA_HEADS, 3), ((0, 0), (0, 0), (0, LANES - 3)))
        o8 = _nsa_sample_sel(idx_flat, pt_flat, qz, nsa_s, win_s, win4, o_cmp, gates8, cache_nsa4,
                             layer=l, dec_batch=dec_batch, n_pages=n_pages, n_top=n_top_s)
        o8 = o8.reshape(dec_batch, NSA_KV_HEADS, NSA_GROUP, NSA_KV_HEADS, HEAD_DIM)
        o_nsa_s = jnp.stack([o8[:, g, :, g, :] for g in range(NSA_KV_HEADS)], axis=1).reshape(dec_batch, NSA_Q_W)
        sq_s = sbb_s[:, :SB_W].reshape(dec_batch, SB_HEADS, 1, HEAD_DIM)
        eye8 = jnp.eye(SB_HEADS, dtype=bool)
        qbd = jnp.where(eye8[None, :, :, None], sq_s, jnp.zeros((), BF16)).reshape(dec_batch, SB_HEADS, SB_W)
        o_sb_s = _sb_sample(pt_flat, qbd, cache_sb4, consts, layer=l, dec_batch=dec_batch, n_pages=n_pages)
        xs = _merge(xs, o_nsa_s.astype(BF16), o_sb_s.reshape(dec_batch, SB_W).astype(BF16), lw, tm=dec_batch)
        xs = _mlp(xs, lw, tm=dec_batch)
        keep_s = min(WINDOW, win_len + 1)
        win_all = jnp.concatenate([state_win[l].reshape(dec_batch, win_len, 2 * NSA_KV_W), win_s[:, None, :]], axis=1)
        outs[3].append(nsa_s.reshape(dec_batch, 1, 4, NSA_KV_HEADS, HEAD_DIM))
        outs[4].append(sb_s.reshape(dec_batch, 1, 2, SB_HEADS, HEAD_DIM))
        outs[5].append(win_all[:, win_len + 1 - keep_s:].reshape(dec_batch, keep_s, 2, NSA_KV_HEADS, HEAD_DIM))

    return (xp.reshape(batch, seq, D_MODEL), xs.reshape(dec_batch, 1, D_MODEL)) + tuple(jnp.stack(o) for o in outs)
```

```python
import functools

import numpy as np
import jax
import jax.numpy as jnp
from jax import lax
from jax.experimental import pallas as pl
from jax.experimental.pallas import tpu as pltpu

F32 = jnp.float32
BF16 = jnp.bfloat16
I32 = jnp.int32

D_MODEL = 1024
HEAD_DIM = 64
NSA_HEADS = 8
NSA_KV_HEADS = 2
NSA_GROUP = NSA_HEADS // NSA_KV_HEADS
CMP_STRIDE = 16
CMP_LEN = 32
CMP_HIDDEN = 128
SLC_BLOCK = 64
SLC_TOP_N = 16
WINDOW = 512
SB_HEADS = 8
D_FF = 4 * D_MODEL
ROPE_THETA = 10000.0
RMS_EPS = 1e-6
FORCE_BONUS = 1e4
PAGE_SIZE = 128
SCALE = HEAD_DIM ** -0.5

LANES = 128
SUBLANES = 8
VMEM_LIMIT_BYTES = 56 * 1024 * 1024

NEG = -1e30
EXP_ZERO_BELOW = -88.0

NSA_Q_W = NSA_HEADS * HEAD_DIM
NSA_KV_W = NSA_KV_HEADS * HEAD_DIM
SB_W = SB_HEADS * HEAD_DIM
ROPED_W = NSA_Q_W + 3 * NSA_KV_W
PLAIN_W = 3 * NSA_KV_W + 3 * SB_W
Q_TILE = 2 * SLC_BLOCK
SB_TILE = 128
SEL_CHUNK = 1024
WIN_KEYS = WINDOW + Q_TILE
CMP_PAGES = 32


def _dot(a, b):
    return jnp.dot(a, b, preferred_element_type=F32)


def _dot_nt(a, b):
    return lax.dot_general(a, b, (((1,), (1,)), ((), ())), preferred_element_type=F32)


def _split2(x):
    hi = x.astype(BF16)
    lo = (x - hi.astype(F32)).astype(BF16)
    return hi, lo


def _split3(x):
    hi = x.astype(BF16)
    r = x - hi.astype(F32)
    mid = r.astype(BF16)
    lo = (r - mid.astype(F32)).astype(BF16)
    return hi, mid, lo


def _rmsnorm(x, gain):
    ms = jnp.mean(x * x, axis=-1, keepdims=True)
    return x * lax.rsqrt(ms + RMS_EPS) * gain


def _softplus(z):
    return jnp.maximum(z, 0.0) + jnp.log(1.0 + jnp.exp(-jnp.abs(z)))


def _masked_softmax_parts(s, mask):
    sm = jnp.where(mask, s, NEG)
    m = jnp.max(sm, axis=-1, keepdims=True)
    p = jnp.where(mask, jnp.exp(sm - m), 0.0)
    den = jnp.maximum(jnp.sum(p, axis=-1, keepdims=True), 1e-30)
    return p, den


def _softmax_biased(s, bias):
    sm = s + bias
    m = jnp.maximum(jnp.max(sm, axis=-1, keepdims=True), 0.5 * NEG)
    p = jnp.exp(sm - m)
    return p, jnp.maximum(_row_sums(p), 1e-30)


def _fold_lanes(x, op):
    out = x[:, 0:LANES]
    for j in range(1, x.shape[1] // LANES):
        out = op(out, x[:, j * LANES:(j + 1) * LANES])
    return out


def _row_sums(x):
    ones = jnp.ones((LANES, LANES), BF16)
    return sum(_dot(part, ones) for part in _split3(_fold_lanes(x, jnp.add)))


def _params(semantics):
    return pltpu.CompilerParams(dimension_semantics=semantics, vmem_limit_bytes=VMEM_LIMIT_BYTES)


def _const_spec(shape):
    n = len(shape)
    return pl.BlockSpec(shape, lambda *_: (0,) * n)


def _proj_kernel(x_ref, gmix_ref, wa_ref, wb_ref, wg_ref, ga_ref, cos_ref, sin_ref, seg_ref,
                 nsa_ref, sbn_ref, win_ref, q_ref, nsab_ref, sbb_ref, gate_ref):
    xn = _rmsnorm(x_ref[...], gmix_ref[...]).astype(BF16)
    ha = _dot(xn, wa_ref[...])
    hb = _dot(xn, wb_ref[...])
    gate_ref[...] = jax.nn.sigmoid(_dot(xn, wg_ref[...]))
    cos = cos_ref[...]
    sin = sin_ref[...]
    seg = seg_ref[...]
    lane = lax.broadcasted_iota(I32, (1, LANES), 1)
    first_half = (lane & (HEAD_DIM - 1)) < (HEAD_DIM // 2)
    roped = []
    for j in range(ROPED_W // LANES):
        h = ha[:, j * LANES:(j + 1) * LANES]
        hi, lo = _split2(h * h)
        ms = _dot(hi, seg) + _dot(lo, seg)
        y = h * lax.rsqrt(ms + RMS_EPS) * ga_ref[:, j * LANES:(j + 1) * LANES]
        rot = jnp.where(first_half, pltpu.roll(y, LANES - HEAD_DIM // 2, 1), pltpu.roll(y, HEAD_DIM // 2, 1))
        roped.append(y * cos + rot * sin)
    nq = NSA_Q_W // LANES
    for j in range(nq):
        q_ref[:, j * LANES:(j + 1) * LANES] = (roped[j] * SCALE).astype(BF16)
    kc, ks, kw = roped[nq], roped[nq + 1], roped[nq + 2]
    vc, vs, vw = hb[:, 0:128], hb[:, 128:256], hb[:, 256:384]
    sq, sk, sv = hb[:, 384:896], hb[:, 896:1408], hb[:, 1408:1920]
    nsa_ref[:, 0:128] = kc
    nsa_ref[:, 128:256] = vc
    nsa_ref[:, 256:384] = ks
    nsa_ref[:, 384:512] = vs
    sbn_ref[:, 0:512] = sk
    sbn_ref[:, 512:1024] = sv
    win_ref[:, 0:128] = kw
    win_ref[:, 128:256] = vw
    nsab_ref[:, 0:128] = ks.astype(BF16)
    nsab_ref[:, 128:256] = vs.astype(BF16)
    nsab_ref[:, 256:384] = kw.astype(BF16)
    nsab_ref[:, 384:512] = vw.astype(BF16)
    sbb_ref[:, 0:512] = (sq * SCALE).astype(BF16)
    sbb_ref[:, 512:1024] = sk.astype(BF16)
    sbb_ref[:, 1024:1536] = sv.astype(BF16)


def _proj(x2, lw, cos, sin, *, tm, pos_blocks):
    m = x2.shape[0]
    row = lambda w: pl.BlockSpec((tm, w), lambda i: (i, 0))
    pos = pl.BlockSpec((tm, LANES), lambda i: (i % pos_blocks, 0))
    in_specs = [row(D_MODEL), _const_spec((1, D_MODEL)), _const_spec((D_MODEL, ROPED_W)),
                _const_spec((D_MODEL, PLAIN_W)), _const_spec((D_MODEL, LANES)), _const_spec((1, ROPED_W)),
                pos, pos, _const_spec((LANES, LANES))]
    widths = [(512, F32), (1024, F32), (256, F32), (512, BF16), (512, BF16), (1536, BF16), (128, F32)]
    return pl.pallas_call(
        _proj_kernel,
        grid=(m // tm,),
        in_specs=in_specs,
        out_specs=[row(w) for w, _ in widths],
        out_shape=[jax.ShapeDtypeStruct((m, w), dt) for w, dt in widths],
        compiler_params=_params(("parallel",)),
    )(x2, lw["gmix"], lw["wa"], lw["wb"], lw["wg"], lw["ga"], cos, sin, lw["seg"])


def _cmp_mlp(load_rows, slot, posa_ref, posb_ref, w1a_ref, w1b_ref, b1_ref, w2_ref):
    acc_a = acc_b = None
    for r in range(0, CMP_STRIDE, 2):
        xr = jnp.concatenate([load_rows(r), load_rows(r + 1)], axis=1)
        pa = _dot((xr + posa_ref[slot, r // 2]).astype(BF16), w1a_ref[slot, r // 2])
        pb = _dot((xr + posb_ref[slot, r // 2]).astype(BF16), w1b_ref[slot, r // 2])
        acc_a = pa if acc_a is None else acc_a + pa
        acc_b = pb if acc_b is None else acc_b + pb
    n = acc_a.shape[0]
    h = acc_a + pltpu.roll(acc_b, n - 1, 0) + b1_ref[slot]
    h = h * jax.nn.sigmoid(h)
    return _dot(h.astype(BF16), w2_ref[slot])


def _cmp_prompt_kernel(xk_ref, xv_ref, posa_ref, posb_ref, w1a_ref, w1b_ref, b1_ref, w2_ref, o_ref):
    nc = o_ref.shape[0]
    for slot, x_ref in enumerate((xk_ref, xv_ref)):
        load = lambda r, x_ref=x_ref: x_ref[pl.ds(r, nc, stride=CMP_STRIDE), :]
        out = _cmp_mlp(load, slot, posa_ref, posb_ref, w1a_ref, w1b_ref, b1_ref, w2_ref)
        o_ref[:, slot * LANES:(slot + 1) * LANES] = out.astype(BF16)


def _cmp_weight_specs():
    half = CMP_STRIDE // 2
    return [_const_spec((2, half, 1, 2 * LANES)), _const_spec((2, half, 1, 2 * LANES)),
            _const_spec((2, half, 2 * LANES, 256)), _const_spec((2, half, 2 * LANES, 256)),
            _const_spec((2, 1, 256)), _const_spec((2, 256, LANES))]


def _cmp_weights(lw):
    return (lw["posa"], lw["posb"], lw["w1a"], lw["w1b"], lw["b1"], lw["w2"])


def _cmp_prompt(nsa_new, lw, *, batch, seq):
    nc = seq // CMP_STRIDE
    return pl.pallas_call(
        _cmp_prompt_kernel,
        grid=(batch,),
        in_specs=[pl.BlockSpec((seq, LANES), lambda b: (b, 0)), pl.BlockSpec((seq, LANES), lambda b: (b, 1))]
        + _cmp_weight_specs(),
        out_specs=pl.BlockSpec((None, nc, 256), lambda b: (b, 0, 0)),
        out_shape=jax.ShapeDtypeStruct((batch, nc, 256), BF16),
        compiler_params=_params(("parallel",)),
    )(nsa_new, nsa_new, *_cmp_weights(lw))


def _cmp_sample_kernel(pt_ref, cache_ref, posa_ref, posb_ref, w1a_ref, w1b_ref, b1_ref, w2_ref, o_ref,
                       buf, rows_ref, sem,
                       *, layer, pages, n_pages, steps, total):
    i = pl.program_id(0)
    per_page = PAGE_SIZE // CMP_STRIDE

    def copies(j, half):
        b = j // steps
        s = j % steps
        out = []
        for k in range(pages + 1):
            p = jnp.minimum(s * pages + k, n_pages - 1)
            src = cache_ref.at[layer, pt_ref[b * n_pages + p], pl.ds(0, 2 * LANES), :]
            out.append(pltpu.make_async_copy(src, buf.at[half, k], sem.at[half]))
        return out

    @pl.when(i == 0)
    def _():
        for c in copies(i, 0):
            c.start()

    @pl.when(i + 1 < total)
    def _():
        for c in copies(i + 1, (i + 1) & 1):
            c.start()

    half = i & 1
    for c in copies(i, half):
        c.wait()
    for slot in range(2):
        for k in range(pages + 1):
            rows_ref[k] = buf[half, k, slot * LANES:(slot + 1) * LANES, :].T

        def load(r):
            rows = [rows_ref[k, pl.ds(r, per_page, stride=CMP_STRIDE), :] for k in range(pages + 1)]
            return jnp.concatenate(rows, axis=0)

        out = _cmp_mlp(load, slot, posa_ref, posb_ref, w1a_ref, w1b_ref, b1_ref, w2_ref)
        o_ref[:, slot * LANES:(slot + 1) * LANES] = out[:pages * per_page].astype(BF16)


def _cmp_sample(cache4, pt_flat, lw, *, layer, dec_batch, n_pages):
    pages = min(CMP_PAGES, n_pages)
    steps = n_pages // pages
    per_page = PAGE_SIZE // CMP_STRIDE
    nc = n_pages * per_page
    total = dec_batch * steps
    wspecs = [pl.BlockSpec(sp.block_shape, lambda i, pt, n=len(sp.block_shape): (0,) * n)
              for sp in _cmp_weight_specs()]
    grid_spec = pltpu.PrefetchScalarGridSpec(
        num_scalar_prefetch=1,
        grid=(total,),
        in_specs=[pl.BlockSpec(memory_space=pl.ANY)] + wspecs,
        out_specs=pl.BlockSpec((None, pages * per_page, 256), lambda i, pt: (i // steps, i % steps, 0)),
        scratch_shapes=[pltpu.VMEM((2, pages + 1, 2 * LANES, PAGE_SIZE), F32),
                        pltpu.VMEM((pages + 1, PAGE_SIZE, LANES), F32),
                        pltpu.SemaphoreType.DMA((2,))],
    )
    kern = functools.partial(_cmp_sample_kernel, layer=layer, pages=pages, n_pages=n_pages, steps=steps, total=total)
    return pl.pallas_call(
        kern,
        grid_spec=grid_spec,
        out_shape=jax.ShapeDtypeStruct((dec_batch, nc, 256), BF16),
        compiler_params=_params(("arbitrary",)),
    )(pt_flat, cache4, *_cmp_weights(lw))


def _beats(bi, key, srow, i):
    tie = jnp.where(srow > i, 1.0, 0.0)
    return jnp.where(bi > key, 1.0, jnp.where(bi == key, tie, 0.0))


def _rank_static(key, srow):
    parts = [jnp.zeros(key.shape, F32) for _ in range(4)]
    for i in range(key.shape[0]):
        parts[i % 4] = parts[i % 4] + _beats(key[i:i + 1, :], key, srow, i)
    return (parts[0] + parts[1]) + (parts[2] + parts[3])


def _forced(srow, cur):
    return (srow == 0) | (srow == cur) | (srow == cur - 1)


def _nsa_prompt_kernel(q_ref, gate_ref, cb_ref, kv_ref, amat_ref, e3_ref, eye_ref, gexp_ref, o_ref, s_ref,
                       *, seq, n_sel, n_top):
    t = pl.program_id(1)
    t0 = t * Q_TILE
    ncb = cb_ref.shape[0]
    rows = NSA_HEADS * Q_TILE
    lane = lax.broadcasted_iota(I32, (1, LANES), 1)
    lane_group = lane >> 6
    qpos = t0 + lax.broadcasted_iota(I32, (Q_TILE, 1), 0)
    q = q_ref[...].astype(F32)
    gates = sum(_dot(part, gexp_ref[...]) for part in _split2(gate_ref[...]))

    def q_slab(g):
        parts = []
        for r in range(NSA_GROUP):
            h = NSA_GROUP * g + r
            blk = q[:, (h // 2) * LANES:(h // 2 + 1) * LANES]
            if h % 2 != g:
                blk = pltpu.roll(blk, HEAD_DIM, 1)
            parts.append(jnp.where(lane_group == g, blk, 0.0))
        return jnp.concatenate(parts, axis=0).astype(BF16)

    q_all = jnp.concatenate([q_slab(g) for g in range(NSA_KV_HEADS)], axis=0)
    per_head = lambda x: jnp.concatenate([x] * NSA_HEADS, axis=0)

    kcb = cb_ref[:, 0:LANES]
    vcb = cb_ref[:, LANES:2 * LANES]
    cb_end = lax.broadcasted_iota(I32, (1, ncb), 1) * CMP_STRIDE + (CMP_LEN - 1)
    p, den = _softmax_biased(_dot_nt(q_all, kcb), per_head(jnp.where(cb_end <= qpos, 0.0, NEG)))
    inv = 1.0 / den
    p = p * jnp.concatenate([inv] * (ncb // LANES), axis=1)
    o_cmp = _dot(p.astype(BF16), vcb)
    imp = jnp.concatenate([sum(p[(NSA_GROUP * g + r) * Q_TILE:(NSA_GROUP * g + r + 1) * Q_TILE]
                               for r in range(NSA_GROUP)) for g in range(NSA_KV_HEADS)], axis=0)
    amat = amat_ref[...]
    imp_s = sum(_dot_nt(amat, part) for part in _split3(imp))
    srow = lax.broadcasted_iota(I32, (n_sel, 1), 0)
    qpos_lane = t0 + (lax.broadcasted_iota(I32, (1, NSA_KV_HEADS * Q_TILE), 1) & (Q_TILE - 1))
    valid = srow * SLC_BLOCK <= qpos_lane
    bonus = jnp.where(_forced(srow, qpos_lane >> 6), FORCE_BONUS, 0.0)
    key = jnp.where(valid, imp_s + bonus, -jnp.inf)
    sel_t = jnp.where(_rank_static(key, srow) < n_top, 1.0, 0.0).astype(BF16)
    sel_q = _dot_nt(eye_ref[...], sel_t).astype(BF16)

    ck = e3_ref.shape[2]
    kcol = lax.broadcasted_iota(I32, (1, ck), 1)
    n_chunks = (t0 + Q_TILE - 1) // ck + 1

    def sel_branch():
        qpos2 = jnp.concatenate([qpos] * NSA_KV_HEADS, axis=0)

        def scores(c, mrun):
            k0 = pl.multiple_of(c * ck, ck)
            allow = (_dot(sel_q, e3_ref[c]) > 0.5) & ((k0 + kcol) <= qpos2)
            bias = jnp.where(allow, 0.0, NEG)
            bias = jnp.concatenate([bias[:Q_TILE]] * NSA_GROUP + [bias[Q_TILE:]] * NSA_GROUP, axis=0)
            s = _dot_nt(q_all, kv_ref[pl.ds(k0, ck), 0:LANES]) + bias
            s_ref[c] = s
            return jnp.maximum(mrun, _fold_lanes(s, jnp.maximum))

        mrun = lax.fori_loop(0, n_chunks, scores, jnp.full((rows, LANES), NEG, F32))
        m = jnp.max(mrun, axis=-1, keepdims=True)

        def weigh(c, carry):
            lrun, acc = carry
            k0 = pl.multiple_of(c * ck, ck)
            p = jnp.exp(s_ref[c] - m)
            acc = acc + _dot(p.astype(BF16), kv_ref[pl.ds(k0, ck), LANES:2 * LANES])
            return lrun + _fold_lanes(p, jnp.add), acc

        zeros = jnp.zeros((rows, LANES), F32)
        lrun, acc = lax.fori_loop(0, n_chunks, weigh, (zeros, zeros))
        return acc / _row_sums(lrun)

    wk = min(WIN_KEYS, seq)
    start = pl.multiple_of(jnp.clip(t0 - WINDOW, 0, seq - wk), SLC_BLOCK)
    kwin = kv_ref[pl.ds(start, wk), 2 * LANES:3 * LANES]
    vwin = kv_ref[pl.ds(start, wk), 3 * LANES:4 * LANES]
    diff = qpos - (start + lax.broadcasted_iota(I32, (1, wk), 1))
    wbias = jnp.where((diff >= 0) & (diff <= WINDOW), 0.0, NEG)
    p, den = _softmax_biased(_dot_nt(q_all, kwin), per_head(wbias))
    o_win = _dot(p.astype(BF16), vwin) / den

    o_sel = sel_branch()
    for g in range(NSA_KV_HEADS):
        heads = []
        for r in range(NSA_GROUP):
            h = NSA_GROUP * g + r
            sl = slice(h * Q_TILE, (h + 1) * Q_TILE)
            gc = [gates[:, (3 * h + c) * LANES:(3 * h + c + 1) * LANES] for c in range(3)]
            oh = gc[0] * o_cmp[sl] + gc[1] * o_sel[sl] + gc[2] * o_win[sl]
            if h % 2 != g:
                oh = pltpu.roll(oh, HEAD_DIM, 1)
            heads.append(oh)
        for pr in range(NSA_GROUP // 2):
            hp = (NSA_GROUP * g) // 2 + pr
            pair = jnp.where(lane_group == 0, heads[2 * pr], heads[2 * pr + 1])
            o_ref[:, hp * LANES:(hp + 1) * LANES] = pair.astype(BF16)


def _nsa_prompt(q, gates, cb, nsab, consts, *, batch, seq):
    nt = seq // Q_TILE
    ncb = seq // CMP_STRIDE
    n_sel = seq // SLC_BLOCK
    ck = min(SEL_CHUNK, seq)
    kern = functools.partial(_nsa_prompt_kernel, seq=seq, n_sel=n_sel, n_top=min(SLC_TOP_N, n_sel))
    tile = lambda w: pl.BlockSpec((Q_TILE, w), lambda b, t: (b * nt + t, 0))
    return pl.pallas_call(
        kern,
        grid=(batch, nt),
        in_specs=[tile(NSA_Q_W), tile(LANES),
                  pl.BlockSpec((None, ncb, 256), lambda b, t: (b, 0, 0)),
                  pl.BlockSpec((seq, 512), lambda b, t: (b, 0)),
                  _const_spec((n_sel, ncb)), _const_spec((seq // ck, n_sel, ck)),
                  _const_spec((NSA_KV_HEADS * Q_TILE, NSA_KV_HEADS * Q_TILE)),
                  _const_spec((LANES, 3 * NSA_HEADS * LANES))],
        out_specs=tile(NSA_Q_W),
        out_shape=jax.ShapeDtypeStruct((batch * seq, NSA_Q_W), BF16),
        scratch_shapes=[pltpu.VMEM((seq // ck, NSA_HEADS * Q_TILE, ck), F32)],
        compiler_params=_params(("parallel", "arbitrary")),
    )(q, gates, cb, nsab, consts["amat_p"], consts["e3"], consts["eye"], consts["gexp"])


def _sb_sums(sps, tri_ones, mask):
    n = sps[0].shape[0]
    parts = []
    for sp in sps:
        lk = -sp if mask is None else jnp.where(mask, -sp, 0.0)
        parts.extend(_split2(lk))
    if n % 16 == 0:
        sums = _dot(jnp.concatenate(parts, axis=0), tri_ones)
        terms = [sums[i * n:(i + 1) * n] for i in range(len(parts))]
    else:
        terms = [_dot(part, tri_ones) for part in parts]
    out = []
    for i in range(len(sps)):
        both = terms[2 * i] + terms[2 * i + 1]
        out.append((both[:, :LANES], both[:, LANES:]))
    return out


def _sb_block(qh, k, v, tri_ones, carry, acc, kv_transposed=False):
    z = _dot(qh, k) if kv_transposed else _dot_nt(qh, k)
    sp = _softplus(z)
    (later, total), = _sb_sums([sp], tri_ones, None)
    a = jnp.exp(z - sp + later + carry).astype(BF16)
    return carry + total, acc + (_dot_nt(a, v) if kv_transposed else _dot(a, v))


def _sb_prompt_kernel(q_ref, k_ref, v_ref, tri_ref, o_ref, qh_ref, carry_ref, acc_ref):
    qi = pl.program_id(1)
    tq = SB_TILE
    tri = tri_ref[...]
    lane_half = lax.broadcasted_iota(I32, (1, LANES), 1) >> 6
    row = lax.broadcasted_iota(I32, (tq, 1), 0)
    col = lax.broadcasted_iota(I32, (1, tq), 1)
    diag_mask = col < row
    for h in range(SB_HEADS):
        qp = q_ref[:, (h // 2) * LANES:(h // 2 + 1) * LANES]
        qh_ref[h] = jnp.where(lane_half == h % 2, qp, jnp.zeros_like(qp))
        carry_ref[h] = jnp.zeros((tq, LANES), F32)
        acc_ref[h] = jnp.zeros((tq, LANES), F32)

    def visit(kb, mask):
        k0 = pl.multiple_of(kb * tq, tq)
        zs = []
        for h in range(SB_HEADS):
            k = k_ref[pl.ds(k0, tq), (h // 2) * LANES:(h // 2 + 1) * LANES]
            zs.append(_dot_nt(qh_ref[h], k))
        sps = [_softplus(z) for z in zs]
        sums = _sb_sums(sps, tri, mask)
        worst = None
        for pair in range(SB_HEADS // 2):
            ws = []
            for h in (2 * pair, 2 * pair + 1):
                a = jnp.exp(zs[h] - sps[h] + sums[h][0] + carry_ref[h])
                ws.append((a if mask is None else jnp.where(mask, a, 0.0)).astype(BF16))
                carry = carry_ref[h] + sums[h][1]
                carry_ref[h] = carry
                worst = carry if worst is None else jnp.maximum(worst, carry)
            v = v_ref[pl.ds(k0, tq), pair * LANES:(pair + 1) * LANES]
            pv = _dot(jnp.concatenate(ws, axis=0), v)
            acc_ref[2 * pair] += pv[:tq]
            acc_ref[2 * pair + 1] += pv[tq:]
        return (jnp.max(worst) > EXP_ZERO_BELOW).astype(I32)

    def cond(st):
        kb, go = st
        return (kb >= 0) & (go > 0)

    def body(st):
        kb, _ = st
        return kb - 1, visit(kb, None)

    lax.while_loop(cond, body, (qi - 1, visit(qi, diag_mask)))
    for pair in range(SB_HEADS // 2):
        o_ref[:, pair * LANES:(pair + 1) * LANES] = jnp.where(
            lane_half == 0, acc_ref[2 * pair], acc_ref[2 * pair + 1]).astype(BF16)


def _sb_prompt(sbb, consts, *, batch, seq):
    nq = seq // SB_TILE
    return pl.pallas_call(
        _sb_prompt_kernel,
        grid=(batch, nq),
        in_specs=[pl.BlockSpec((SB_TILE, SB_W), lambda b, qi: (b * nq + qi, 0)),
                  pl.BlockSpec((seq, SB_W), lambda b, qi: (b, 1)),
                  pl.BlockSpec((seq, SB_W), lambda b, qi: (b, 2)),
                  _const_spec((SB_TILE, SB_TILE + LANES))],
        out_specs=pl.BlockSpec((SB_TILE, SB_W), lambda b, qi: (b * nq + qi, 0)),
        out_shape=jax.ShapeDtypeStruct((batch * seq, SB_W), BF16),
        scratch_shapes=[pltpu.VMEM((SB_HEADS, SB_TILE, LANES), BF16),
                        pltpu.VMEM((SB_HEADS, SB_TILE, LANES), F32),
                        pltpu.VMEM((SB_HEADS, SB_TILE, LANES), F32)],
        compiler_params=_params(("parallel", "arbitrary")),
    )(sbb, sbb, sbb, consts["tri"])


def _merge_kernel(x_ref, on_ref, os_ref, gmix_ref, wgate_ref, bgate_ref, wbn_ref, wbs_ref, wout_ref, o_ref):
    x = x_ref[...]
    xn = _rmsnorm(x, gmix_ref[...]).astype(BF16)
    g = jax.nn.sigmoid(_dot(xn, wgate_ref[...]) + bgate_ref[...])
    u = g[:, :D_MODEL] * _dot(on_ref[...], wbn_ref[...]) + g[:, D_MODEL:] * _dot(os_ref[...], wbs_ref[...])
    o_ref[...] = x + _dot(u.astype(BF16), wout_ref[...])


def _mlp_kernel(x_ref, gmlp_ref, wup_ref, wdn_ref, o_ref):
    x = x_ref[...]
    h = _rmsnorm(x, gmlp_ref[...]).astype(BF16)
    acc = None
    step = 1024
    for c in range(D_FF // step):
        up = _dot(h, wup_ref[:, c * step:(c + 1) * step])
        up = jnp.square(jnp.maximum(up, 0.0)).astype(BF16)
        part = _dot(up, wdn_ref[c * step:(c + 1) * step, :])
        acc = part if acc is None else acc + part
    o_ref[...] = x + acc


def _resident(shape):
    n = len(shape)
    return pl.BlockSpec(shape, lambda *_: (0,) * n, pipeline_mode=pl.Buffered(1))


def _merge(x2, o_nsa, o_sb, lw, *, tm):
    m = x2.shape[0]
    row = lambda w: pl.BlockSpec((tm, w), lambda i: (i, 0))
    return pl.pallas_call(
        _merge_kernel,
        grid=(m // tm,),
        in_specs=[row(D_MODEL), row(NSA_Q_W), row(SB_W), _resident((1, D_MODEL)),
                  _resident((D_MODEL, 2 * D_MODEL)), _resident((1, 2 * D_MODEL)),
                  _resident((NSA_Q_W, D_MODEL)), _resident((SB_W, D_MODEL)), _resident((D_MODEL, D_MODEL))],
        out_specs=row(D_MODEL),
        out_shape=jax.ShapeDtypeStruct((m, D_MODEL), F32),
        compiler_params=_params(("parallel",)),
    )(x2, o_nsa, o_sb, lw["gmix"], lw["wgate"], lw["bgate"], lw["wbn"], lw["wbs"], lw["wout"])


def _mlp(x2, lw, *, tm):
    m = x2.shape[0]
    row = pl.BlockSpec((tm, D_MODEL), lambda i: (i, 0))
    return pl.pallas_call(
        _mlp_kernel,
        grid=(m // tm,),
        in_specs=[row, _resident((1, D_MODEL)), _resident((D_MODEL, D_FF)), _resident((D_FF, D_MODEL))],
        out_specs=row,
        out_shape=jax.ShapeDtypeStruct((m, D_MODEL), F32),
        compiler_params=_params(("parallel",)),
    )(x2, lw["gmlp"], lw["wup"], lw["wdn"])


def _nsa_sample_cmp_kernel(qz_ref, cb_ref, amat_ref, oc_ref, imps_ref, *, q_pos):
    qz = qz_ref[...]
    nc = cb_ref.shape[0]
    kcb = cb_ref[:, 0:LANES]
    vcb = cb_ref[:, LANES:2 * LANES]
    cb_end = lax.broadcasted_iota(I32, (1, nc), 1) * CMP_STRIDE + (CMP_LEN - 1)
    p, den = _masked_softmax_parts(_dot_nt(qz, kcb), cb_end <= q_pos)
    p = p / den
    oc_ref[...] = _dot(p.astype(BF16), vcb)
    row = lax.broadcasted_iota(I32, (NSA_HEADS, 1), 0)
    g0 = jnp.sum(p[0:NSA_GROUP], axis=0, keepdims=True)
    g1 = jnp.sum(p[NSA_GROUP:], axis=0, keepdims=True)
    imp = jnp.where(row < NSA_GROUP, g0, g1)
    amat = amat_ref[...]
    imps_ref[...] = sum(_dot(part, amat) for part in _split3(imp))


def _nsa_sample_cmp(qz, cb, consts, *, dec_batch, q_pos):
    nc = cb.shape[1]
    ns_pad = consts["amat_s"].shape[1]
    return pl.pallas_call(
        functools.partial(_nsa_sample_cmp_kernel, q_pos=q_pos),
        grid=(dec_batch,),
        in_specs=[pl.BlockSpec((None, NSA_HEADS, LANES), lambda b: (b, 0, 0)),
                  pl.BlockSpec((None, nc, 256), lambda b: (b, 0, 0)),
                  _const_spec((nc, ns_pad))],
        out_specs=[pl.BlockSpec((None, NSA_HEADS, LANES), lambda b: (b, 0, 0)),
                   pl.BlockSpec((None, NSA_HEADS, ns_pad), lambda b: (b, 0, 0))],
        out_shape=[jax.ShapeDtypeStruct((dec_batch, NSA_HEADS, LANES), F32),
                   jax.ShapeDtypeStruct((dec_batch, NSA_HEADS, ns_pad), F32)],
        compiler_params=_params(("parallel",)),
    )(qz, cb, consts["amat_s"])


def _topk_sample_kernel(imp_ref, idx_ref, key_ref, *, n_sel, n_top):
    nrows = imp_ref.shape[0]
    srow = lax.broadcasted_iota(I32, (nrows, 1), 0)
    bonus = jnp.where(_forced(srow, n_sel - 1), FORCE_BONUS, 0.0)
    key = jnp.where(srow < n_sel, imp_ref[...] + bonus, -jnp.inf)
    key_ref[...] = key

    def body(i, cnt):
        return cnt + _beats(key_ref[pl.ds(i, 1), :], key, srow, i)

    cnt = lax.fori_loop(0, n_sel, body, jnp.zeros(key.shape, F32))
    srow_f = srow.astype(F32)
    picks = [jnp.sum(jnp.where(cnt == float(k), srow_f, 0.0), axis=0, keepdims=True) for k in range(n_top)]
    idx_ref[...] = jnp.concatenate(picks, axis=0).astype(I32)


def _topk_sample(imp_t, *, n_sel, n_top):
    nrows = imp_t.shape[0]
    return pl.pallas_call(
        functools.partial(_topk_sample_kernel, n_sel=n_sel, n_top=n_top),
        out_shape=jax.ShapeDtypeStruct((n_top, LANES), I32),
        scratch_shapes=[pltpu.VMEM((nrows, LANES), F32)],
    )(imp_t)


def _nsa_sample_sel_kernel(idx_ref, pt_ref, qz_ref, own_ref, wown_ref, win_ref, oc_ref, gate_ref, cache_ref,
                           o_ref, buf, sem, *, layer, n_pages, n_top):
    b = pl.program_id(0)
    per_page = PAGE_SIZE // SLC_BLOCK
    n_past_blk = n_pages * per_page

    def blk_of(g, k):
        return idx_ref[(b * NSA_KV_HEADS + g) * n_top + k]

    def copy(g, k):
        blk = jnp.minimum(blk_of(g, k), n_past_blk - 1)
        page = pt_ref[b * n_pages + blk // per_page]
        src = cache_ref.at[layer, page, pl.ds(2 * LANES, 2 * LANES), :]
        return pltpu.make_async_copy(src, buf.at[g, k], sem.at[g, k])

    for g in range(NSA_KV_HEADS):
        for k in range(n_top):
            copy(g, k).start()

    qz = qz_ref[...]
    qf = qz.astype(F32)
    row = lax.broadcasted_iota(I32, (NSA_HEADS, 1), 0)
    top_rows = row < NSA_GROUP
    n_keys = n_top * PAGE_SIZE
    key_lane = lax.broadcasted_iota(I32, (1, n_keys), 1)
    kslot = key_lane >> 7
    khalf = (key_lane >> 6) & (per_page - 1)

    for g in range(NSA_KV_HEADS):
        for k in range(n_top):
            copy(g, k).wait()

    s_parts, m_parts, own_parts, v_parts = [], [], [], []
    for g in range(NSA_KV_HEADS):
        kt = jnp.concatenate([buf[g, k, 0:LANES, :] for k in range(n_top)], axis=1).astype(BF16)
        v_parts.append(jnp.concatenate([buf[g, k, LANES:2 * LANES, :] for k in range(n_top)], axis=1).astype(BF16))
        s_parts.append(_dot(qz, kt))
        msk = jnp.zeros((1, n_keys), I32)
        has_own = jnp.int32(0)
        for k in range(n_top):
            blk = blk_of(g, k)
            in_past = (blk < n_past_blk).astype(I32)
            msk = jnp.where(kslot == k, jnp.where(khalf == blk % per_page, in_past, 0), msk)
            has_own = jnp.maximum(has_own, 1 - in_past)
        m_parts.append(msk)
        own_parts.append(has_own)
    s = jnp.where(top_rows, s_parts[0], s_parts[1])
    mask = jnp.where(top_rows, m_parts[0], m_parts[1]) > 0
    own_ok = jnp.where(top_rows, own_parts[0], own_parts[1]) > 0
    own = own_ref[...]
    s_own = jnp.sum(qf * own[:, 2 * LANES:3 * LANES], axis=-1, keepdims=True)
    sm = jnp.where(mask, s, NEG)
    m = jnp.maximum(jnp.max(sm, axis=-1, keepdims=True), jnp.where(own_ok, s_own, NEG))
    p = jnp.where(mask, jnp.exp(sm - m), 0.0)
    p_own = jnp.where(own_ok, jnp.exp(s_own - m), 0.0)
    den = jnp.maximum(jnp.sum(p, axis=-1, keepdims=True) + p_own, 1e-30)
    pb = p.astype(BF16)
    o_sel = jnp.where(top_rows, _dot_nt(pb, v_parts[0]), _dot_nt(pb, v_parts[1]))
    o_sel = (o_sel + p_own * own[:, 3 * LANES:4 * LANES]) / den

    wown = wown_ref[...]
    kwin = win_ref[0:LANES, :].astype(BF16)
    vwin = win_ref[LANES:2 * LANES, :].astype(BF16)
    sw = _dot(qz, kwin)
    sw_own = jnp.sum(qf * wown[:, 0:LANES], axis=-1, keepdims=True)
    mw = jnp.maximum(jnp.max(sw, axis=-1, keepdims=True), sw_own)
    pw = jnp.exp(sw - mw)
    pw_own = jnp.exp(sw_own - mw)
    denw = jnp.sum(pw, axis=-1, keepdims=True) + pw_own
    o_win = (_dot_nt(pw.astype(BF16), vwin) + pw_own * wown[:, LANES:2 * LANES]) / denw

    gates = gate_ref[...]
    o_ref[...] = gates[:, 0:1] * oc_ref[...] + gates[:, 1:2] * o_sel + gates[:, 2:3] * o_win


def _nsa_sample_sel(idx_flat, pt_flat, qz, nsa_new, win_new, win_t, o_cmp, gates8, cache_t,
                    *, layer, dec_batch, n_pages, n_top):
    win_len = win_t.shape[3]
    per_b = lambda shape: pl.BlockSpec((None,) + shape, lambda b, *_: (b,) + (0,) * len(shape))
    grid_spec = pltpu.PrefetchScalarGridSpec(
        num_scalar_prefetch=2,
        grid=(dec_batch,),
        in_specs=[per_b((NSA_HEADS, LANES)), per_b((1, 512)), per_b((1, 256)),
                  pl.BlockSpec((None, None, 2 * LANES, win_len), lambda b, *_: (layer, b, 0, 0)),
                  per_b((NSA_HEADS, LANES)), per_b((NSA_HEADS, LANES)),
                  pl.BlockSpec(memory_space=pl.ANY)],
        out_specs=per_b((NSA_HEADS, LANES)),
        scratch_shapes=[pltpu.VMEM((NSA_KV_HEADS, n_top, 2 * LANES, PAGE_SIZE), F32),
                        pltpu.SemaphoreType.DMA((NSA_KV_HEADS, n_top))],
    )
    return pl.pallas_call(
        functools.partial(_nsa_sample_sel_kernel, layer=layer, n_pages=n_pages, n_top=n_top),
        grid_spec=grid_spec,
        out_shape=jax.ShapeDtypeStruct((dec_batch, NSA_HEADS, LANES), F32),
        compiler_params=_params(("arbitrary",)),
    )(idx_flat, pt_flat, qz, nsa_new.reshape(dec_batch, 1, 512), win_new.reshape(dec_batch, 1, 256), win_t,
      o_cmp, gates8, cache_t)


def _sb_sample_kernel(pt_ref, qbd_ref, tri_ref, bd_ref, cache_ref, o_ref, buf, sem, *, layer, n_pages):
    b = pl.program_id(0)

    def copy(p, slot):
        return pltpu.make_async_copy(cache_ref.at[layer, pt_ref[b * n_pages + p]], buf.at[slot], sem.at[slot])

    copy(n_pages - 1, 0).start()
    qbd = qbd_ref[...]
    tri = tri_ref[...]

    def cond(st):
        p, _, go, _, _ = st
        return (p >= 0) & (go > 0)

    def body(st):
        p, it, _, carry, acc = st
        slot = it & 1

        @pl.when(p >= 1)
        def _():
            copy(p - 1, 1 - slot).start()

        copy(p, slot).wait()
        k = buf[slot, 0:SB_W, :].astype(BF16)
        v = buf[slot, SB_W:2 * SB_W, :].astype(BF16)
        carry, acc = _sb_block(qbd, k, v, tri, carry, acc, kv_transposed=True)
        go = (jnp.max(carry) > EXP_ZERO_BELOW).astype(I32)
        return p - 1, it + 1, go, carry, acc

    init = (jnp.int32(n_pages - 1), jnp.int32(0), jnp.int32(1),
            jnp.zeros((SB_HEADS, LANES), F32), jnp.zeros((SB_HEADS, SB_W), F32))
    p_end, it_end, _, _, acc = lax.while_loop(cond, body, init)

    @pl.when(p_end >= 0)
    def _():
        copy(p_end, it_end & 1).wait()

    o_ref[...] = jnp.sum(acc * bd_ref[...], axis=0, keepdims=True)


def _sb_sample(pt_flat, qbd, cache4, consts, *, layer, dec_batch, n_pages):
    grid_spec = pltpu.PrefetchScalarGridSpec(
        num_scalar_prefetch=1,
        grid=(dec_batch,),
        in_specs=[pl.BlockSpec((None, SB_HEADS, SB_W), lambda b, pt: (b, 0, 0)),
                  pl.BlockSpec((PAGE_SIZE, PAGE_SIZE + LANES), lambda b, pt: (0, 0)),
                  pl.BlockSpec((SB_HEADS, SB_W), lambda b, pt: (0, 0)),
                  pl.BlockSpec(memory_space=pl.ANY)],
        out_specs=pl.BlockSpec((None, 1, SB_W), lambda b, pt: (b, 0, 0)),
        scratch_shapes=[pltpu.VMEM((2, 2 * SB_W, PAGE_SIZE), F32), pltpu.SemaphoreType.DMA((2,))],
    )
    return pl.pallas_call(
        functools.partial(_sb_sample_kernel, layer=layer, n_pages=n_pages),
        grid_spec=grid_spec,
        out_shape=jax.ShapeDtypeStruct((dec_batch, 1, SB_W), F32),
        compiler_params=_params(("arbitrary",)),
    )(pt_flat, qbd, consts["tri"], consts["bd"], cache4)


def _rope_tables(pos):
    half = HEAD_DIM // 2
    inv = ROPE_THETA ** (-jnp.arange(half, dtype=F32) / half)
    ang = pos.astype(F32)[:, None] * inv[None, :]
    c, s = jnp.cos(ang), jnp.sin(ang)
    return jnp.tile(c, (1, 4)), jnp.tile(jnp.concatenate([-s, s], axis=1), (1, 2))


def _importance_matrix(n_blk, n_sel):
    ratio = SLC_BLOCK // CMP_STRIDE
    j = np.arange(n_blk)[:, None]
    lo = np.arange(n_sel)[None, :] * ratio
    a = ((j >= lo) & (j < lo + ratio)).astype(np.float32) + ((j + 1 >= lo) & (j + 1 < lo + ratio)).astype(np.float32)
    return a


def _constants(seq, past_len):
    n_sel = seq // SLC_BLOCK
    ncb = seq // CMP_STRIDE
    ck = min(SEL_CHUNK, seq)
    amat_p = _importance_matrix(ncb, n_sel)
    amat_p[ncb - 1:] = 0.0
    key_blk = np.arange(seq) // SLC_BLOCK
    e = (key_blk[None, :] == np.arange(n_sel)[:, None]).astype(np.float32)
    e3 = e.reshape(n_sel, seq // ck, ck).transpose(1, 0, 2)
    nc_s = past_len // CMP_STRIDE
    n_sel_s = past_len // SLC_BLOCK + 1
    ns_pad = -(-n_sel_s // LANES) * LANES
    amat_s = np.zeros((nc_s, ns_pad), np.float32)
    amat_s[:, :n_sel_s] = _importance_matrix(nc_s, n_sel_s)
    amat_s[nc_s - 1:] = 0.0
    idx = np.arange(SB_TILE)
    return {
        "amat_p": jnp.asarray(amat_p.T, BF16),
        "e3": jnp.asarray(e3, BF16),
        "eye": jnp.asarray(np.eye(NSA_KV_HEADS * Q_TILE, dtype=np.float32), BF16),
        "gexp": jnp.asarray(np.kron(np.eye(LANES, 3 * NSA_HEADS, dtype=np.float32), np.ones((1, LANES), np.float32)), BF16),
        "amat_s": jnp.asarray(amat_s, BF16),
        "tri": jnp.asarray(np.concatenate([(idx[:, None] > idx[None, :]).astype(np.float32),
                                           np.ones((SB_TILE, LANES), np.float32)], axis=1), BF16),
        "bd": jnp.asarray((np.arange(SB_W)[None, :] // HEAD_DIM == np.arange(SB_HEADS)[:, None]).astype(np.float32)),
    }


def _layer_weights(l, norm_mix, norm_mlp, w_in, q_norm, k_norm, cmp_pos, cmp_w1, cmp_b1, cmp_w2,
                   w_br_nsa, w_br_sb, w_gate, b_gate, w_out, w_up, w_down):
    w = w_in[l]
    c = np.cumsum([NSA_Q_W] + [NSA_KV_W] * 6 + [3 * NSA_HEADS] + [SB_W] * 3).tolist()
    col = lambda i: w[:, (0 if i == 0 else c[i - 1]):c[i]]
    wa = jnp.concatenate([col(0), col(1), col(3), col(5)], axis=1).astype(BF16)
    wb = jnp.concatenate([col(2), col(4), col(6), col(8), col(9), col(10)], axis=1).astype(BF16)
    wg = jnp.pad(col(7), ((0, 0), (0, LANES - 3 * NSA_HEADS))).astype(BF16)
    ga = jnp.concatenate([jnp.tile(q_norm[l], NSA_HEADS)] + [jnp.tile(k_norm[l, i], NSA_KV_HEADS) for i in range(3)])
    seg = np.kron(np.eye(LANES // HEAD_DIM), np.full((HEAD_DIM, HEAD_DIM), 1.0 / HEAD_DIM)).astype(np.float32)
    eye2 = jnp.eye(NSA_KV_HEADS, dtype=F32)
    w1 = cmp_w1[l].reshape(2, CMP_LEN, HEAD_DIM, CMP_HIDDEN)
    expand1 = lambda t: jnp.einsum("srdh,gk->srgdkh", t, eye2).reshape(2, CMP_STRIDE // 2, 2 * LANES, 256).astype(BF16)
    pos = cmp_pos[l]
    expand_pos = lambda t: jnp.broadcast_to(t[:, :, None, None, :], (2, CMP_STRIDE, 1, NSA_KV_HEADS, HEAD_DIM)
                                            ).reshape(2, CMP_STRIDE // 2, 1, 2 * LANES)
    return {
        "gmix": norm_mix[l][None, :], "gmlp": norm_mlp[l][None, :],
        "wa": wa, "wb": wb, "wg": wg, "ga": ga[None, :], "seg": jnp.asarray(seg, BF16),
        "w1a": expand1(w1[:, :CMP_STRIDE]), "w1b": expand1(w1[:, CMP_STRIDE:]),
        "posa": expand_pos(pos[:, :CMP_STRIDE]), "posb": expand_pos(pos[:, CMP_STRIDE:]),
        "b1": jnp.broadcast_to(cmp_b1[l][:, None, :], (2, NSA_KV_HEADS, CMP_HIDDEN)).reshape(2, 1, 256),
        "w2": jnp.einsum("shd,gk->sghkd", cmp_w2[l], eye2).reshape(2, 256, LANES).astype(BF16),
        "wgate": w_gate[l].astype(BF16), "bgate": b_gate[l][None, :],
        "wbn": w_br_nsa[l].astype(BF16), "wbs": w_br_sb[l].astype(BF16), "wout": w_out[l].astype(BF16),
        "wup": w_up[l].astype(BF16), "wdn": w_down[l].astype(BF16),
    }


def _row_tile(m, pref):
    tm = min(pref, m)
    assert m % tm == 0
    return tm


def kernel(x_prompt, x_sample, cache_nsa, cache_sb, state_win, page_table, norm_mix, norm_mlp, w_in, q_norm, k_norm,
           cmp_pos, cmp_w1, cmp_b1, cmp_w2, w_br_nsa, w_br_sb, w_gate, b_gate, w_out, w_up, w_down):
    batch, seq, _ = x_prompt.shape
    dec_batch, dec_seq, _ = x_sample.shape
    depth = w_in.shape[0]
    n_pages = page_table.shape[1]
    past_len = n_pages * PAGE_SIZE
    win_len = state_win.shape[2]
    n_phys = cache_nsa.shape[1]
    assert dec_seq == 1 and win_len <= WINDOW and win_len <= past_len
    assert seq % SEL_CHUNK == 0 and seq % (CMP_STRIDE * LANES) == 0
    assert seq % SB_TILE == 0 and seq >= WIN_KEYS and n_pages % min(CMP_PAGES, n_pages) == 0

    consts = _constants(seq, past_len)
    cos_p, sin_p = _rope_tables(jnp.arange(seq, dtype=I32))
    cos_s, sin_s = _rope_tables(jnp.full((dec_batch,), past_len, I32))
    rows_last = (0, 1, 3, 4, 5, 2)
    cache_nsa4 = jnp.transpose(cache_nsa, rows_last).reshape(depth, n_phys, 4 * NSA_KV_W, PAGE_SIZE)
    cache_sb4 = jnp.transpose(cache_sb, rows_last).reshape(depth, n_phys, 2 * SB_W, PAGE_SIZE)
    win4 = jnp.transpose(state_win, rows_last).reshape(depth, dec_batch, 2 * NSA_KV_W, win_len)
    pt_flat = page_table.reshape(-1).astype(I32)
    n_sel_s = past_len // SLC_BLOCK + 1
    n_top_s = min(SLC_TOP_N, n_sel_s)
    ns_pad = consts["amat_s"].shape[1]
    tm_p = _row_tile(batch * seq, 512)
    tm_pos = _row_tile(seq, tm_p)
    assert tm_pos == tm_p

    xp = x_prompt.reshape(batch * seq, D_MODEL)
    xs = x_sample.reshape(dec_batch, D_MODEL)
    outs = [[] for _ in range(6)]
    for l in range(depth):
        lw = _layer_weights(l, norm_mix, norm_mlp, w_in, q_norm, k_norm, cmp_pos, cmp_w1, cmp_b1, cmp_w2,
                            w_br_nsa, w_br_sb, w_gate, b_gate, w_out, w_up, w_down)
        nsa_new, sb_new, win_new, q, nsab, sbb, gates = _proj(xp, lw, cos_p, sin_p, tm=tm_p, pos_blocks=seq // tm_p)
        cb = _cmp_prompt(nsa_new, lw, batch=batch, seq=seq)
        o_nsa = _nsa_prompt(q, gates, cb, nsab, consts, batch=batch, seq=seq)
        o_sb = _sb_prompt(sbb, consts, batch=batch, seq=seq)
        xp = _merge(xp, o_nsa, o_sb, lw, tm=tm_p)
        xp = _mlp(xp, lw, tm=tm_p)
        keep = min(WINDOW, seq)
        outs[0].append(nsa_new.reshape(batch, seq, 4, NSA_KV_HEADS, HEAD_DIM))
        outs[1].append(sb_new.reshape(batch, seq, 2, SB_HEADS, HEAD_DIM))
        outs[2].append(win_new.reshape(batch, seq, 2, NSA_KV_HEADS, HEAD_DIM)[:, seq - keep:])

        nsa_s, sb_s, win_s, q_s, _, sbb_s, gates_s = _proj(xs, lw, cos_s, sin_s, tm=dec_batch, pos_blocks=1)
        cb_s = _cmp_sample(cache_nsa4, pt_flat, lw, layer=l, dec_batch=dec_batch, n_pages=n_pages)
        head_group = jnp.arange(NSA_HEADS) // NSA_GROUP
        on_group = head_group[:, None] == jnp.arange(NSA_KV_HEADS)[None, :]
        qz = jnp.where(on_group[None, :, :, None], q_s.reshape(dec_batch, NSA_HEADS, 1, HEAD_DIM),
                       jnp.zeros((), BF16)).reshape(dec_batch, NSA_HEADS, LANES)
        o_cmp, imp_s = _nsa_sample_cmp(qz, cb_s, consts, dec_batch=dec_batch, q_pos=past_len)
        imp_g = imp_s[:, ::NSA_GROUP, :].reshape(dec_batch * NSA_KV_HEADS, ns_pad)
        imp_t = jnp.pad(imp_g.T, ((0, 0), (0, LANES - dec_batch * NSA_KV_HEADS)))
        idx = _topk_sample(imp_t, n_sel=n_sel_s, n_top=n_top_s)
        idx_flat = idx[:, :dec_batch * NSA_KV_HEADS].T.reshape(-1)
        gates8 = jnp.pad(gates_s[:, :3 * NSA_HEADS].reshape(dec_batch, NSA_HEADS, 3), ((0, 0), (0, 0), (0, LANES - 3)))
        o8 = _nsa_sample_sel(idx_flat, pt_flat, qz, nsa_s, win_s, win4, o_cmp, gates8, cache_nsa4,
                             layer=l, dec_batch=dec_batch, n_pages=n_pages, n_top=n_top_s)
        o8 = o8.reshape(dec_batch, NSA_KV_HEADS, NSA_GROUP, NSA_KV_HEADS, HEAD_DIM)
        o_nsa_s = jnp.stack([o8[:, g, :, g, :] for g in range(NSA_KV_HEADS)], axis=1).reshape(dec_batch, NSA_Q_W)
        sq_s = sbb_s[:, :SB_W].reshape(dec_batch, SB_HEADS, 1, HEAD_DIM)
        eye8 = jnp.eye(SB_HEADS, dtype=bool)
        qbd = jnp.where(eye8[None, :, :, None], sq_s, jnp.zeros((), BF16)).reshape(dec_batch, SB_HEADS, SB_W)
        o_sb_s = _sb_sample(pt_flat, qbd, cache_sb4, consts, layer=l, dec_batch=dec_batch, n_pages=n_pages)
        xs = _merge(xs, o_nsa_s.astype(BF16), o_sb_s.reshape(dec_batch, SB_W).astype(BF16), lw, tm=dec_batch)
        xs = _mlp(xs, lw, tm=dec_batch)
        keep_s = min(WINDOW, win_len + 1)
        win_all = jnp.concatenate([state_win[l].reshape(dec_batch, win_len, 2 * NSA_KV_W), win_s[:, None, :]], axis=1)
        outs[3].append(nsa_s.reshape(dec_batch, 1, 4, NSA_KV_HEADS, HEAD_DIM))
        outs[4].append(sb_s.reshape(dec_batch, 1, 2, SB_HEADS, HEAD_DIM))
        outs[5].append(win_all[:, win_len + 1 - keep_s:].reshape(dec_batch, keep_s, 2, NSA_KV_HEADS, HEAD_DIM))

    return (xp.reshape(batch, seq, D_MODEL), xs.reshape(dec_batch, 1, D_MODEL)) + tuple(jnp.stack(o) for o in outs)
```
